```python
import jax, jax.numpy as jnp
from jax import lax
import numpy as np

D_MODEL = 1024
BATCH = 4
SEQ = 4096
DEPTH = 2

N_META = 16
HG_HEADS = 8
HG_DK = 128
HG_DV = D_MODEL // HG_HEADS
HG_DIM_K = HG_HEADS * HG_DK
HG_DIM_V = HG_HEADS * HG_DV
CHUNK = 16
PAD_FRONT = (-N_META) % CHUNK
CONV_DIM = D_MODEL
CONV_WIDTH = 31
D_FF = 2816
N_EXPERTS = 8
TOP_K = 2
D_FF_EXPERT = 3584
N_DENSE = (DEPTH + 1) // 2
N_MOE = DEPTH // 2
DN_ALPHA = (2 * DEPTH) ** 0.25
DN_BETA = (8 * DEPTH) ** -0.25
LN_EPS = 1e-5
RMS_EPS = 1e-6
OFF_Q = 0
OFF_F = OFF_Q + HG_DIM_K
OFF_I = OFF_F + HG_DIM_K
OFF_OG = OFF_I + HG_DIM_V
OFF_CV = OFF_OG + HG_DIM_V
OFF_CG = OFF_CV + CONV_DIM
OFF_GA = OFF_CG + CONV_DIM
OFF_GB = OFF_GA + D_MODEL
IN_COLS = OFF_GB + D_MODEL

kernel_name = "hgrn2_conformer_gated_moe_deepnorm"


def layer_norm(x, g, b):
    xf = x.astype(jnp.float32)
    mu = jnp.mean(xf, axis=-1, keepdims=True)
    var = jnp.mean(jnp.square(xf - mu), axis=-1, keepdims=True)
    y = (xf - mu) * lax.rsqrt(var + LN_EPS) * g.astype(jnp.float32) + b.astype(jnp.float32)
    return y.astype(x.dtype)


def hgrn2_recurrence(q_raw, f_raw, i_raw, lb):
    B, L, _ = q_raw.shape
    f32 = jnp.float32
    z = f_raw.astype(f32)
    lb = lb.astype(f32)
    q = jax.nn.silu(q_raw.astype(f32))
    has_lb = lb > 0
    lb_safe = jnp.where(has_lb, lb, 1.0)
    log_sz = jax.nn.log_sigmoid(z)
    log_f = jnp.where(has_lb, jnp.logaddexp(log_sz, jnp.log(lb_safe) + jax.nn.log_sigmoid(-z)), log_sz)
    k = (1.0 - lb) * jax.nn.sigmoid(-z)
    v = i_raw.astype(f32)
    pad = ((0, 0), (PAD_FRONT, 0), (0, 0))
    q, k, v, log_f = [jnp.pad(a, pad) for a in (q, k, v, log_f)]
    n = (L + PAD_FRONT) // CHUNK

    def to_chunks(a, d):
        return a.reshape(B, n, CHUNK, HG_HEADS, d).transpose(0, 3, 1, 2, 4)

    q, k, log_f = to_chunks(q, HG_DK), to_chunks(k, HG_DK), to_chunks(log_f, HG_DK)
    v = to_chunks(v, HG_DV)
    b = jnp.cumsum(log_f, axis=3)
    b_last = b[:, :, :, -1:, :]
    causal = jnp.tril(jnp.ones((CHUNK, CHUNK), dtype=bool))[:, :, None]
    diff = b[:, :, :, :, None, :] - b[:, :, :, None, :, :]
    decay = jnp.where(causal, jnp.exp(jnp.where(causal, diff, 0.0)), 0.0)
    att = jnp.einsum('bhntd,bhntsd,bhnsd->bhnts', q, decay, k)
    o_intra = jnp.einsum('bhnts,bhnsv->bhntv', att, v)
    k_end = k * jnp.exp(b_last - b)
    d_state = jnp.einsum('bhnsd,bhnsv->bhndv', k_end, v)
    chunk_decay = jnp.exp(b_last[:, :, :, 0, :])

    def step(state, inp):
        dec, ds = inp
        return dec[..., None] * state + ds, state

    s0 = jnp.zeros((B, HG_HEADS, HG_DK, HG_DV), f32)
    _, s_prev = lax.scan(step, s0, (jnp.moveaxis(chunk_decay, 2, 0), jnp.moveaxis(d_state, 2, 0)))
    s_prev = jnp.moveaxis(s_prev, 0, 2)
    o_inter = jnp.einsum('bhntd,bhndv->bhntv', q * jnp.exp(b), s_prev)
    o = (o_intra + o_inter).transpose(0, 2, 3, 1, 4).reshape(B, n * CHUNK, HG_HEADS, HG_DV)
    return o[:, PAD_FRONT:]


def conformer_conv(u_val, u_gate, conv_w, conv_b, ln_g, ln_b, w_pw):
    u = u_val * jax.nn.sigmoid(u_gate)
    y = lax.conv_general_dilated(
        u, conv_w[:, None, :], window_strides=(1,), padding=[(CONV_WIDTH - 1, 0)],
        dimension_numbers=('NWC', 'WIO', 'NWC'), feature_group_count=CONV_DIM) + conv_b
    y = jax.nn.silu(layer_norm(y, ln_g, ln_b))
    return y @ w_pw


def swiglu(x, w_gate, w_up, w_down):
    return (jax.nn.silu(x @ w_gate) * (x @ w_up)) @ w_down


def moe_swiglu(x, w_router, w_gate, w_up, w_down):
    B, L, D = x.shape
    t = x.reshape(B * L, D)
    logits = (t @ w_router).astype(jnp.float32)
    top_val, top_idx = lax.top_k(logits, TOP_K)
    top_w = jax.nn.softmax(top_val, axis=-1)
    gates = jnp.sum(jax.nn.one_hot(top_idx, N_EXPERTS, dtype=jnp.float32) * top_w[..., None], axis=1)
    gates = gates.astype(t.dtype)
    out = jnp.zeros_like(t)
    for e in range(N_EXPERTS):
        out = out + gates[:, e:e + 1] * swiglu(t, w_gate[e], w_up[e], w_down[e])
    return out.reshape(B, L, D)


def setup_inputs(seed: int = 0) -> dict:
    key = jax.random.key(seed)
    ks = jax.random.split(key, 32)
    nrm = lambda k, shape, scale: jax.random.normal(k, shape, jnp.float32) * scale
    gain = lambda k, shape: 1.0 + 0.02 * jax.random.normal(k, shape, jnp.float32)
    col_scale = jnp.concatenate([
        jnp.ones((2 * HG_DIM_K,), jnp.float32),
        jnp.full((HG_DIM_V,), DN_BETA, jnp.float32),
        jnp.ones((HG_DIM_V,), jnp.float32),
        jnp.full((CONV_DIM,), DN_BETA, jnp.float32),
        jnp.ones((CONV_DIM + 2 * D_MODEL,), jnp.float32)])
    return {
        "x": nrm(ks[0], (BATCH, SEQ, D_MODEL), 1.0),
        "meta_tokens": nrm(ks[1], (N_META, D_MODEL), 1.0),
        "ln_in_g": gain(ks[2], (D_MODEL,)),
        "ln_in_b": nrm(ks[3], (D_MODEL,), 0.02),
        "w_in": nrm(ks[4], (DEPTH, D_MODEL, IN_COLS), D_MODEL ** -0.5) * col_scale,
        "lower_bounds": nrm(ks[5], (DEPTH, HG_DIM_K), 0.5),
        "hg_norm_g": gain(ks[6], (DEPTH, HG_DIM_V)),
        "w_hg_out": nrm(ks[7], (DEPTH, HG_DIM_V, D_MODEL), HG_DIM_V ** -0.5 * DN_BETA),
        "conv_w": nrm(ks[8], (DEPTH, CONV_WIDTH, CONV_DIM), CONV_WIDTH ** -0.5),
        "conv_b": nrm(ks[9], (DEPTH, CONV_DIM), 0.02),
        "conv_ln_g": gain(ks[10], (DEPTH, CONV_DIM)),
        "conv_ln_b": nrm(ks[11], (DEPTH, CONV_DIM), 0.02),
        "w_conv_out": nrm(ks[12], (DEPTH, CONV_DIM, D_MODEL), CONV_DIM ** -0.5 * DN_BETA),
        "w_out": nrm(ks[13], (DEPTH, D_MODEL, D_MODEL), D_MODEL ** -0.5 * DN_BETA),
        "ln1_g": gain(ks[14], (DEPTH, D_MODEL)),
        "ln1_b": nrm(ks[15], (DEPTH, D_MODEL), 0.02),
        "ffn_w_gate": nrm(ks[16], (N_DENSE, D_MODEL, D_FF), D_MODEL ** -0.5 * DN_BETA),
        "ffn_w_up": nrm(ks[17], (N_DENSE, D_MODEL, D_FF), D_MODEL ** -0.5 * DN_BETA),
        "ffn_w_down": nrm(ks[18], (N_DENSE, D_FF, D_MODEL), D_FF ** -0.5 * DN_BETA),
        "moe_router": nrm(ks[19], (N_MOE, D_MODEL, N_EXPERTS), D_MODEL ** -0.5),
        "moe_w_gate": nrm(ks[20], (N_MOE, N_EXPERTS, D_MODEL, D_FF_EXPERT), D_MODEL ** -0.5 * DN_BETA),
        "moe_w_up": nrm(ks[21], (N_MOE, N_EXPERTS, D_MODEL, D_FF_EXPERT), D_MODEL ** -0.5 * DN_BETA),
        "moe_w_down": nrm(ks[22], (N_MOE, N_EXPERTS, D_FF_EXPERT, D_MODEL), D_FF_EXPERT ** -0.5 * DN_BETA),
        "ln2_g": gain(ks[23], (DEPTH, D_MODEL)),
        "ln2_b": nrm(ks[24], (DEPTH, D_MODEL), 0.02),
    }


def reference(x, meta_tokens, ln_in_g, ln_in_b, w_in, lower_bounds, hg_norm_g, w_hg_out,
              conv_w, conv_b, conv_ln_g, conv_ln_b, w_conv_out, w_out, ln1_g, ln1_b,
              ffn_w_gate, ffn_w_up, ffn_w_down, moe_router, moe_w_gate, moe_w_up, moe_w_down,
              ln2_g, ln2_b):
    B = x.shape[0]
    meta = jnp.broadcast_to(meta_tokens[None].astype(x.dtype), (B, N_META, D_MODEL))
    h = layer_norm(jnp.concatenate([meta, x], axis=1), ln_in_g, ln_in_b)
    L = h.shape[1]
    p = jax.nn.softmax(lower_bounds.astype(jnp.float32), axis=0)
    lb_all = jnp.cumsum(p, axis=0) - p[0]
    for l in range(DEPTH):
        proj = h @ w_in[l]
        o = hgrn2_recurrence(proj[..., OFF_Q:OFF_F], proj[..., OFF_F:OFF_I],
                             proj[..., OFF_I:OFF_OG], lb_all[l])
        o = o * lax.rsqrt(jnp.mean(jnp.square(o), axis=-1, keepdims=True) + RMS_EPS)
        o = o.reshape(B, L, HG_DIM_V) * hg_norm_g[l].astype(jnp.float32)
        o = (o * jax.nn.silu(proj[..., OFF_OG:OFF_CV].astype(jnp.float32))).astype(h.dtype)
        y_rec = o @ w_hg_out[l]
        y_conv = conformer_conv(proj[..., OFF_CV:OFF_CG], proj[..., OFF_CG:OFF_GA],
                                conv_w[l], conv_b[l], conv_ln_g[l], conv_ln_b[l], w_conv_out[l])
        mixed = (jax.nn.sigmoid(proj[..., OFF_GA:OFF_GB]) * y_rec
                 + jax.nn.sigmoid(proj[..., OFF_GB:IN_COLS]) * y_conv)
        h = layer_norm(DN_ALPHA * h + mixed @ w_out[l], ln1_g[l], ln1_b[l])
        if l % 2 == 0:
            j = l // 2
            f = swiglu(h, ffn_w_gate[j], ffn_w_up[j], ffn_w_down[j])
        else:
            j = l // 2
            f = moe_swiglu(h, moe_router[j], moe_w_gate[j], moe_w_up[j], moe_w_down[j])
        h = layer_norm(DN_ALPHA * h + f, ln2_g[l], ln2_b[l])
    return h[:, N_META:]
```

```python
import functools

import jax
import jax.numpy as jnp
from jax import lax
from jax.experimental import pallas as pl
from jax.experimental.pallas import tpu as pltpu

F32 = jnp.float32
BF16 = jnp.bfloat16

N_META = 16
HG_HEADS = 8
HG_DK = 128
CONV_WIDTH = 31
N_EXPERTS = 8
LN_EPS = 1e-5
RMS_EPS = 1e-6

LANES = 128
SUBLANES = 8
TL = 256
HIST = 32
VMEM_LIMIT = 56 * 1024 * 1024


def _sigmoid(x):
    return 0.5 * jnp.tanh(0.5 * x) + 0.5


def _layer_norm(x, g, b):
    mu = jnp.mean(x, axis=-1, keepdims=True)
    xc = x - mu
    var = jnp.mean(xc * xc, axis=-1, keepdims=True)
    return xc * lax.rsqrt(var + LN_EPS) * g + b


def _dot(a, b):
    return jnp.dot(a, b, preferred_element_type=F32)


def _dot_nt(a, b):
    return lax.dot_general(a, b, (((1,), (1,)), ((), ())), preferred_element_type=F32)


def _dot_tn(a, b):
    return lax.dot_general(a, b, (((0,), (0,)), ((), ())), preferred_element_type=F32)


def _const_spec(shape):
    nd = len(shape)
    return pl.BlockSpec(shape, lambda *_: (0,) * nd, pipeline_mode=pl.Buffered(1))


def _ln_in_kernel(x_ref, meta_ref, g_ref, b_ref, out_ref, *, pad):
    j = pl.program_id(1)

    @pl.when(j == 0)
    def _():
        rows = lax.broadcasted_iota(jnp.int32, (TL, 1), 0)
        y = _layer_norm(meta_ref[...], g_ref[...], b_ref[...])
        out_ref[...] = jnp.where(rows >= pad, y, 0.0)

    @pl.when(j > 0)
    def _():
        out_ref[...] = _layer_norm(x_ref[...], g_ref[...], b_ref[...])


def _ln_in(x, meta_tokens, g, b):
    B, S, D = x.shape
    nt = 1 + S // TL
    pad = TL - N_META
    meta_pad = jnp.pad(meta_tokens.astype(F32), ((pad, 0), (0, 0)))
    return pl.pallas_call(
        functools.partial(_ln_in_kernel, pad=pad),
        grid=(B, nt),
        in_specs=[
            pl.BlockSpec((None, TL, D), lambda bi, j: (bi, jnp.maximum(j - 1, 0), 0)),
            _const_spec((TL, D)),
            _const_spec((1, D)),
            _const_spec((1, D)),
        ],
        out_specs=pl.BlockSpec((None, TL, D), lambda bi, j: (bi, j, 0)),
        out_shape=jax.ShapeDtypeStruct((B, nt * TL, D), F32),
        compiler_params=pltpu.CompilerParams(
            dimension_semantics=("arbitrary", "arbitrary")),
        name="ln_in",
    )(x, meta_pad, g.reshape(1, D), b.reshape(1, D))


def _cumsum_rows(x):
    n = x.shape[0]
    rows = lax.broadcasted_iota(jnp.int32, (n, 1), 0)
    s = 1
    while s < n:
        x = x + jnp.where(rows >= s, pltpu.roll(x, s, 0), 0.0)
        s *= 2
    return x


def _level_scores(qh, kh, bh, m):
    n = qh.shape[0]
    pieces = []
    for jb in range(n // (2 * m)):
        lo = jb * 2 * m
        e = bh[lo + m - 1:lo + m, :]
        pieces.append(bh[lo:lo + 2 * m, :] - e)
    d = pieces[0] if len(pieces) == 1 else jnp.concatenate(pieces, axis=0)
    w = jnp.exp(-jnp.abs(d))
    rows = lax.broadcasted_iota(jnp.int32, (n, 1), 0)
    odd = (rows & m) != 0
    qp = jnp.where(odd, qh * w, 0.0).astype(BF16)
    kp = jnp.where(odd, 0.0, kh * w).astype(BF16)
    return _dot_nt(qp, kp)


def _mix_kernel(h_ref, w_in_ref, lb_ref, hgg_ref, w_hg_ref, cw_ref, cb_ref, clg_ref, clb_ref,
                w_cv_ref, w_out_ref, ln_g_ref, ln_b_ref, out_ref,
                st_s, hist_s, hb_s, q_s, k_s, b_s, v_s, og_s, ob_s, attd_s,
                *, pad, alpha):
    D = h_ref.shape[-1]
    H = HG_HEADS
    HALF = TL // 2
    j = pl.program_id(1)

    @pl.when(j == 0)
    def _():
        st_s[...] = jnp.zeros_like(st_s)
        hist_s[0:HIST, :] = jnp.zeros((HIST, D), F32)

    rows = j * TL + lax.broadcasted_iota(jnp.int32, (TL, 1), 0)
    hv = jnp.where(rows >= pad, h_ref[...], 0.0)
    hb_s[...] = hv.astype(BF16)

    def proj(g):
        return _dot(hb_s[...], w_in_ref[:, g * D:(g + 1) * D])

    def split_heads(dst, val):
        for hd in range(H):
            dst[hd] = val[:, hd * LANES:(hd + 1) * LANES]

    p = proj(0)
    split_heads(q_s, p * _sigmoid(p))

    z = proj(1)
    lb = lb_ref[...]
    e = jnp.exp(-jnp.abs(z))
    r = 1.0 / (1.0 + e)
    er = e * r
    pos = z >= 0.0
    sz = jnp.where(pos, r, er)
    snz = jnp.where(pos, er, r)
    log_sz = jnp.minimum(z, 0.0) - jnp.log(1.0 + e)
    log_f = jnp.where(lb > 0.0, jnp.log(lb + (1.0 - lb) * sz), log_sz)
    split_heads(k_s, (1.0 - lb) * snz)
    split_heads(b_s, _cumsum_rows(log_f))

    split_heads(v_s, proj(2))
    p = proj(3)
    split_heads(og_s, p * _sigmoid(p))
    hist_s[HIST:HIST + TL, :] = proj(4) * _sigmoid(proj(5))

    ri = lax.broadcasted_iota(jnp.int32, (HALF, HALF), 0)
    ci = lax.broadcasted_iota(jnp.int32, (HALF, HALF), 1)
    lane8 = lax.broadcasted_iota(jnp.int32, (SUBLANES, LANES), 1)
    sub8 = lax.broadcasted_iota(jnp.int32, (SUBLANES, LANES), 0)

    def head_body(hd, carry):
        q = q_s[hd]
        k = k_s[hd]
        bc = b_s[hd]
        vb = v_s[hd].astype(BF16)
        st = st_s[hd]
        b_last = bc[TL - 1:TL, :]

        o_inter = _dot_nt((q * jnp.exp(bc)).astype(BF16), st.astype(BF16))
        k_end = (k * jnp.exp(b_last - bc)).astype(BF16)
        st_s[hd] = st * jnp.exp(b_last) + _dot_tn(vb, k_end)

        def diag_body(i, c):
            r0 = pl.multiple_of(i * SUBLANES, SUBLANES)
            q8 = q_s[hd, pl.ds(r0, SUBLANES), :]
            k8 = k_s[hd, pl.ds(r0, SUBLANES), :]
            b8 = b_s[hd, pl.ds(r0, SUBLANES), :]
            c0 = r0 % HALF
            acc = jnp.zeros((SUBLANES, LANES), F32)
            for s in range(SUBLANES):
                dec = jnp.exp(jnp.minimum(b8 - b8[s:s + 1, :], 0.0))
                col = jnp.sum(q8 * dec * k8[s:s + 1, :], axis=-1, keepdims=True)
                acc = jnp.where(lane8 == c0 + s, col, acc)
            keep = (lane8 >= c0) & (lane8 - c0 <= sub8)
            attd_s[pl.ds(r0, SUBLANES), :] = jnp.where(keep, acc, 0.0)
            return c

        lax.fori_loop(0, TL // SUBLANES, diag_body, 0)

        w_top = jnp.exp(-jnp.abs(bc - bc[HALF - 1:HALF, :]))
        a10 = _dot_nt((q[HALF:] * w_top[HALF:]).astype(BF16),
                      (k[:HALF] * w_top[:HALF]).astype(BF16))
        atts = []
        for hh in range(2):
            sl = slice(hh * HALF, (hh + 1) * HALF)
            qh, kh, bh = q[sl], k[sl], bc[sl]
            att = attd_s[sl, :]
            m = HALF // 2
            while m >= SUBLANES:
                sib = (ri & -(2 * m)) == (ci & -(2 * m))
                att = att + jnp.where(sib, _level_scores(qh, kh, bh, m), 0.0)
                m //= 2
            atts.append(att)
        o0 = _dot(atts[0].astype(BF16), vb[:HALF])
        o1 = _dot(jnp.concatenate([a10, atts[1]], axis=1).astype(BF16), vb)
        o = o_inter + jnp.concatenate([o0, o1], axis=0)

        ms = jnp.mean(o * o, axis=-1, keepdims=True)
        ob_s[hd] = (o * lax.rsqrt(ms + RMS_EPS) * hgg_ref[hd] * og_s[hd]).astype(BF16)
        return carry

    lax.fori_loop(0, H, head_body, 0)

    y_rec = _dot(jnp.concatenate([ob_s[hd] for hd in range(H)], axis=1), w_hg_ref[...])

    base = HIST - (CONV_WIDTH - 1)
    strips = []
    for c in range(D // LANES):
        cs = slice(c * LANES, (c + 1) * LANES)
        acc = jnp.zeros((TL, LANES), F32) + cb_ref[:, cs]
        for t in range(CONV_WIDTH):
            acc = acc + hist_s[base + t:base + t + TL, cs] * cw_ref[t:t + 1, cs]
        strips.append(acc)
    yc = jnp.concatenate(strips, axis=1)
    hist_s[0:HIST, :] = hist_s[TL:TL + HIST, :]
    yc = _layer_norm(yc, clg_ref[...], clb_ref[...])
    y_conv = _dot((yc * _sigmoid(yc)).astype(BF16), w_cv_ref[...])

    mixed = _sigmoid(proj(6)) * y_rec + _sigmoid(proj(7)) * y_conv
    res = alpha * hv + _dot(mixed.astype(BF16), w_out_ref[...])
    out_ref[...] = _layer_norm(res, ln_g_ref[...], ln_b_ref[...])


def _mix_layer(h, w_in, lb, hg_g, w_hg, conv_w, conv_b, cln_g, cln_b, w_cv, w_out, ln_g, ln_b,
               *, alpha):
    B, Lp, D = h.shape
    H = HG_HEADS
    nt = Lp // TL
    row = lambda a: a.reshape(1, -1).astype(F32)
    return pl.pallas_call(
        functools.partial(_mix_kernel, pad=TL - N_META, alpha=alpha),
        grid=(B, nt),
        in_specs=[
            pl.BlockSpec((None, TL, D), lambda bi, j: (bi, j, 0)),
            _const_spec(w_in.shape),
            _const_spec((1, D)),
            _const_spec((H, 1, LANES)),
            _const_spec((D, D)),
            _const_spec((CONV_WIDTH, D)),
            _const_spec((1, D)),
            _const_spec((1, D)),
            _const_spec((1, D)),
            _const_spec((D, D)),
            _const_spec((D, D)),
            _const_spec((1, D)),
            _const_spec((1, D)),
        ],
        out_specs=pl.BlockSpec((None, TL, D), lambda bi, j: (bi, j, 0)),
        out_shape=jax.ShapeDtypeStruct((B, Lp, D), F32),
        scratch_shapes=[
            pltpu.VMEM((H, HG_DK, HG_DK), F32),
            pltpu.VMEM((HIST + TL, D), F32),
            pltpu.VMEM((TL, D), BF16),
            pltpu.VMEM((H, TL, LANES), F32),
            pltpu.VMEM((H, TL, LANES), F32),
            pltpu.VMEM((H, TL, LANES), F32),
            pltpu.VMEM((H, TL, LANES), F32),
            pltpu.VMEM((H, TL, LANES), F32),
            pltpu.VMEM((H, TL, LANES), BF16),
            pltpu.VMEM((TL, LANES), F32),
        ],
        compiler_params=pltpu.CompilerParams(
            dimension_semantics=("arbitrary", "arbitrary"),
            vmem_limit_bytes=VMEM_LIMIT),
        name="mix_layer",
    )(h, w_in.astype(BF16), row(lb), hg_g.reshape(H, 1, LANES).astype(F32), w_hg.astype(BF16),
      conv_w.astype(F32), row(conv_b), row(cln_g), row(cln_b), w_cv.astype(BF16),
      w_out.astype(BF16), row(ln_g), row(ln_b))


def _ffn_kernel(h_ref, wg_ref, wu_ref, wd_ref, g_ref, b_ref, out_ref, a_s, *, alpha, fc):
    hv = h_ref[...]
    xb = hv.astype(BF16)
    F = wg_ref.shape[1]
    for c in range(F // fc):
        cs = slice(c * fc, (c + 1) * fc)
        g = _dot(xb, wg_ref[:, cs])
        u = _dot(xb, wu_ref[:, cs])
        a_s[:, cs] = (g * _sigmoid(g) * u).astype(BF16)
    res = alpha * hv + _dot(a_s[...], wd_ref[...])
    out_ref[...] = _layer_norm(res, g_ref[...], b_ref[...])


def _ffn_chunk(F):
    for fc in (512, 1408, 1024, 256, 128):
        if F % fc == 0:
            return fc
    return F


def _dense_ffn(h, wg, wu, wd, ln_g, ln_b, *, alpha):
    B, Lp, D = h.shape
    F = wg.shape[1]
    R = B * Lp
    out = pl.pallas_call(
        functools.partial(_ffn_kernel, alpha=alpha, fc=_ffn_chunk(F)),
        grid=(R // TL,),
        in_specs=[
            pl.BlockSpec((TL, D), lambda i: (i, 0)),
            _const_spec((D, F)),
            _const_spec((D, F)),
            _const_spec((F, D)),
            _const_spec((1, D)),
            _const_spec((1, D)),
        ],
        out_specs=pl.BlockSpec((TL, D), lambda i: (i, 0)),
        out_shape=jax.ShapeDtypeStruct((R, D), F32),
        scratch_shapes=[pltpu.VMEM((TL, F), BF16)],
        compiler_params=pltpu.CompilerParams(
            dimension_semantics=("arbitrary",), vmem_limit_bytes=VMEM_LIMIT),
        name="dense_ffn",
    )(h.reshape(R, D), wg.astype(BF16), wu.astype(BF16), wd.astype(BF16),
      ln_g.reshape(1, D).astype(F32), ln_b.reshape(1, D).astype(F32))
    return out.reshape(B, Lp, D)


def _router_kernel(h_ref, wr_ref, gate_ref, rank_ref, cnt_s):
    first = (pl.program_id(0) == 0) & (pl.program_id(1) == 0)

    @pl.when(first)
    def _():
        cnt_s[...] = jnp.zeros_like(cnt_s)

    logits = jnp.dot(h_ref[...], wr_ref[...], preferred_element_type=F32,
                     precision=lax.Precision.HIGHEST)
    lane = lax.broadcasted_iota(jnp.int32, (TL, LANES), 1).astype(F32)
    neg = jnp.float32(-jnp.inf)
    logits = jnp.where(lane < N_EXPERTS, logits, neg)
    m1 = jnp.max(logits, axis=-1, keepdims=True)
    i1 = jnp.min(jnp.where(logits == m1, lane, float(LANES)), axis=-1, keepdims=True)
    oh1 = lane == i1
    rest = jnp.where(oh1, neg, logits)
    m2 = jnp.max(rest, axis=-1, keepdims=True)
    i2 = jnp.min(jnp.where(rest == m2, lane, float(LANES)), axis=-1, keepdims=True)
    oh2 = lane == i2
    e = jnp.exp(m2 - m1)
    w1 = 1.0 / (1.0 + e)
    w2 = e * w1
    gates = jnp.where(oh1, w1, 0.0) + jnp.where(oh2, w2, 0.0)
    gate_ref[...] = gates
    sel = (gates > 0.0).astype(BF16)
    ri = lax.broadcasted_iota(jnp.int32, (TL, TL), 0)
    ci = lax.broadcasted_iota(jnp.int32, (TL, TL), 1)
    before = _dot((ci < ri).astype(BF16), sel)
    cnt = cnt_s[...]
    rank_ref[...] = (before + cnt).astype(jnp.int32)
    cnt_s[...] = cnt + before[TL - 1:TL, :] + sel[TL - 1:TL, :].astype(F32)


def _router(h, w_router, seq):
    B, Lp, D = h.shape
    nj = seq // TL
    T = B * seq
    wr = jnp.pad(w_router.astype(F32), ((0, 0), (0, LANES - N_EXPERTS)))
    return pl.pallas_call(
        _router_kernel,
        grid=(B, nj),
        in_specs=[
            pl.BlockSpec((None, TL, D), lambda bi, j: (bi, j + 1, 0)),
            _const_spec((D, LANES)),
        ],
        out_specs=[
            pl.BlockSpec((TL, LANES), lambda bi, j: (bi * nj + j, 0)),
            pl.BlockSpec((TL, LANES), lambda bi, j: (bi * nj + j, 0)),
        ],
        out_shape=[jax.ShapeDtypeStruct((T, LANES), F32),
                   jax.ShapeDtypeStruct((T, LANES), jnp.int32)],
        scratch_shapes=[pltpu.VMEM((1, LANES), F32)],
        compiler_params=pltpu.CompilerParams(
            dimension_semantics=("arbitrary", "arbitrary")),
        name="moe_router",
    )(h, wr)


def _pair_list(lo, hi, tstart_of_group, blk_of_group, n_max):
    nonempty = hi > lo
    t_lo = lo // TL
    t_hi = jnp.where(nonempty, (hi - 1) // TL, t_lo - 1)
    npair = t_hi - t_lo + 1
    end = jnp.cumsum(npair)
    total = end[-1]
    start = end - npair
    p = jnp.arange(n_max, dtype=jnp.int32)
    pc = jnp.minimum(p, total - 1)
    g = jnp.searchsorted(end, pc, side="right").astype(jnp.int32)
    tile = tstart_of_group[g] + t_lo[g] + (pc - start[g])
    blk = blk_of_group[g]
    valid = (p < total).astype(jnp.int32)
    return tile.astype(jnp.int32), blk.astype(jnp.int32), valid


def _dispatch_kernel(tile_ref, blk_ref, valid_ref, first_ref, texp_ref,
                     h_ref, slot_ref, xs_ref):
    p = pl.program_id(0)

    @pl.when(first_ref[p] == 1)
    def _():
        xs_ref[...] = jnp.zeros_like(xs_ref)

    @pl.when(valid_ref[p] == 1)
    def _():
        tile = tile_ref[p]
        e = texp_ref[tile]
        srow = slot_ref[pl.ds(e, 1), :] - tile * TL
        ri = lax.broadcasted_iota(jnp.int32, (TL, TL), 0)
        onehot = (ri == srow).astype(BF16)
        xs_ref[...] += _dot(onehot, h_ref[...].astype(BF16)).astype(BF16)


def _expert_up_kernel(texp_ref, nused_ref, x_ref, wg_ref, wu_ref, a_ref, *, fc):
    i = pl.program_id(0)

    @pl.when(i < nused_ref[0])
    def _():
        xb = x_ref[...]
        F = wg_ref.shape[1]
        for c in range(F // fc):
            cs = slice(c * fc, (c + 1) * fc)
            g = _dot(xb, wg_ref[:, cs])
            u = _dot(xb, wu_ref[:, cs])
            a_ref[:, cs] = (g * _sigmoid(g) * u).astype(BF16)


def _expert_down_kernel(texp_ref, nused_ref, a_ref, wd_ref, y_ref):
    i = pl.program_id(0)

    @pl.when(i < nused_ref[0])
    def _():
        y_ref[...] = _dot(a_ref[...], wd_ref[...]).astype(BF16)


def _combine_kernel(tile_ref, blk_ref, valid_ref, first_ref, last_ref, texp_ref,
                    y_ref, slot_ref, gate_ref, h_ref, g_ref, b_ref, out_ref, acc_s, *, alpha):
    p = pl.program_id(0)

    @pl.when(first_ref[p] == 1)
    def _():
        acc_s[...] = jnp.zeros_like(acc_s)

    @pl.when(valid_ref[p] == 1)
    def _():
        tile = tile_ref[p]
        e = texp_ref[tile]
        lane = lax.broadcasted_iota(jnp.int32, (TL, LANES), 1)
        pick = lane == e
        scol = jnp.sum(jnp.where(pick, slot_ref[...].astype(F32), 0.0), axis=-1, keepdims=True)
        gcol = jnp.sum(jnp.where(pick, gate_ref[...], 0.0), axis=-1, keepdims=True)
        ci = lax.broadcasted_iota(jnp.int32, (TL, TL), 1)
        onehot = (ci == scol.astype(jnp.int32) - tile * TL).astype(BF16)
        acc_s[...] += gcol * _dot(onehot, y_ref[...])

    @pl.when(last_ref[p] == 1)
    def _():
        res = alpha * h_ref[...] + acc_s[...]
        out_ref[...] = _layer_norm(res, g_ref[...], b_ref[...])


def _moe_layer(h, w_router, wg, wu, wd, ln_g, ln_b, seq, *, alpha):
    B, Lp, D = h.shape
    E = N_EXPERTS
    F = wg.shape[-1]
    nj = seq // TL
    T = B * seq
    NB = T // TL
    NT = (2 * T) // TL + E
    NP = NB * E + NT

    gate, rank = _router(h, w_router, seq)

    gate8 = gate[:, :E]
    rank8 = rank[:, :E]
    sel8 = gate8 > 0.0
    counts = rank8[-1] + sel8[-1].astype(jnp.int32)
    ntile = (counts + TL - 1) // TL
    tstart = jnp.cumsum(ntile) - ntile
    n_used = jnp.sum(ntile).astype(jnp.int32).reshape(1)
    slot8 = jnp.where(sel8, tstart[None, :] * TL + rank8, -1).astype(jnp.int32)
    slot_tm = jnp.pad(slot8, ((0, 0), (0, LANES - E)), constant_values=-1)
    slot_em = slot8.reshape(NB, TL, E).transpose(0, 2, 1)
    texp = jnp.clip(jnp.searchsorted(jnp.cumsum(ntile), jnp.arange(NT), side="right"),
                    0, E - 1).astype(jnp.int32)

    cb = jnp.concatenate([rank8[::TL], counts[None, :]], axis=0)
    lo_be, hi_be = cb[:-1], cb[1:]
    blk_ids = jnp.arange(NB, dtype=jnp.int32)
    d_tile, d_blk, d_valid = _pair_list(
        lo_be.T.reshape(-1), hi_be.T.reshape(-1), jnp.repeat(tstart, NB),
        jnp.tile(blk_ids, E), NP)
    d_first = jnp.concatenate([jnp.ones((1,), jnp.int32),
                               (d_tile[1:] != d_tile[:-1]).astype(jnp.int32)]) * d_valid
    c_tile, c_blk, c_valid = _pair_list(
        lo_be.reshape(-1), hi_be.reshape(-1), jnp.tile(tstart, NB),
        jnp.repeat(blk_ids, E), NP)
    chg = (c_blk[1:] != c_blk[:-1]).astype(jnp.int32)
    c_first = jnp.concatenate([jnp.ones((1,), jnp.int32), chg]) * c_valid
    c_next_valid = jnp.concatenate([c_valid[1:], jnp.zeros((1,), jnp.int32)])
    c_last = jnp.concatenate([chg, jnp.ones((1,), jnp.int32)])
    c_last = jnp.where(c_next_valid == 0, 1, c_last) * c_valid

    hblk = lambda blk: (blk // nj, blk % nj + 1, 0)

    xs = pl.pallas_call(
        _dispatch_kernel,
        grid_spec=pltpu.PrefetchScalarGridSpec(
            num_scalar_prefetch=5,
            grid=(NP,),
            in_specs=[
                pl.BlockSpec((None, TL, D), lambda p, t, b, v, f, x: hblk(b[p])),
                pl.BlockSpec((None, E, TL), lambda p, t, b, v, f, x: (b[p], 0, 0)),
            ],
            out_specs=pl.BlockSpec((TL, D), lambda p, t, b, v, f, x: (t[p], 0)),
        ),
        out_shape=jax.ShapeDtypeStruct((NT * TL, D), BF16),
        compiler_params=pltpu.CompilerParams(dimension_semantics=("arbitrary",)),
        name="moe_dispatch",
    )(d_tile, d_blk, d_valid, d_first, texp, h, slot_em)

    used = lambda i, n: jnp.minimum(i, n[0] - 1)
    act = pl.pallas_call(
        functools.partial(_expert_up_kernel, fc=_ffn_chunk(F)),
        grid_spec=pltpu.PrefetchScalarGridSpec(
            num_scalar_prefetch=2,
            grid=(NT,),
            in_specs=[
                pl.BlockSpec((TL, D), lambda i, x, n: (used(i, n), 0)),
                pl.BlockSpec((None, D, F), lambda i, x, n: (x[used(i, n)], 0, 0)),
                pl.BlockSpec((None, D, F), lambda i, x, n: (x[used(i, n)], 0, 0)),
            ],
            out_specs=pl.BlockSpec((TL, F), lambda i, x, n: (used(i, n), 0)),
        ),
        out_shape=jax.ShapeDtypeStruct((NT * TL, F), BF16),
        compiler_params=pltpu.CompilerParams(
            dimension_semantics=("arbitrary",), vmem_limit_bytes=VMEM_LIMIT),
        name="moe_expert_up",
    )(texp, n_used, xs, wg.astype(BF16), wu.astype(BF16))

    ys = pl.pallas_call(
        _expert_down_kernel,
        grid_spec=pltpu.PrefetchScalarGridSpec(
            num_scalar_prefetch=2,
            grid=(NT,),
            in_specs=[
                pl.BlockSpec((TL, F), lambda i, x, n: (used(i, n), 0)),
                pl.BlockSpec((None, F, D), lambda i, x, n: (x[used(i, n)], 0, 0)),
            ],
            out_specs=pl.BlockSpec((TL, D), lambda i, x, n: (used(i, n), 0)),
        ),
        out_shape=jax.ShapeDtypeStruct((NT * TL, D), BF16),
        compiler_params=pltpu.CompilerParams(
            dimension_semantics=("arbitrary",), vmem_limit_bytes=VMEM_LIMIT),
        name="moe_expert_down",
    )(texp, n_used, act, wd.astype(BF16))

    out = pl.pallas_call(
        functools.partial(_combine_kernel, alpha=alpha),
        grid_spec=pltpu.PrefetchScalarGridSpec(
            num_scalar_prefetch=6,
            grid=(NP,),
            in_specs=[
                pl.BlockSpec((TL, D), lambda p, t, b, v, f, l, x: (t[p], 0)),
                pl.BlockSpec((TL, LANES), lambda p, t, b, v, f, l, x: (b[p], 0)),
                pl.BlockSpec((TL, LANES), lambda p, t, b, v, f, l, x: (b[p], 0)),
                pl.BlockSpec((None, TL, D), lambda p, t, b, v, f, l, x: hblk(b[p])),
                pl.BlockSpec((1, D), lambda p, t, b, v, f, l, x: (0, 0)),
                pl.BlockSpec((1, D), lambda p, t, b, v, f, l, x: (0, 0)),
            ],
            out_specs=pl.BlockSpec((None, TL, D),
                                   lambda p, t, b, v, f, l, x: (b[p] // nj, b[p] % nj, 0)),
            scratch_shapes=[pltpu.VMEM((TL, D), F32)],
        ),
        out_shape=jax.ShapeDtypeStruct((B, seq, D), F32),
        compiler_params=pltpu.CompilerParams(dimension_semantics=("arbitrary",)),
        name="moe_combine",
    )(c_tile, c_blk, c_valid, c_first, c_last, texp, ys, slot_tm, gate, h,
      ln_g.reshape(1, D).astype(F32), ln_b.reshape(1, D).astype(F32))
    return out


def kernel(x, meta_tokens, ln_in_g, ln_in_b, w_in, lower_bounds, hg_norm_g, w_hg_out, conv_w,
           conv_b, conv_ln_g, conv_ln_b, w_conv_out, w_out, ln1_g, ln1_b, ffn_w_gate, ffn_w_up,
           ffn_w_down, moe_router, moe_w_gate, moe_w_up, moe_w_down, ln2_g, ln2_b):
    B, S, D = x.shape
    depth = w_in.shape[0]
    assert S % TL == 0 and D % LANES == 0 and meta_tokens.shape[0] == N_META
    alpha = float((2 * depth) ** 0.25)

    pr = jax.nn.softmax(lower_bounds.astype(F32), axis=0)
    lb_all = jnp.cumsum(pr, axis=0) - pr[0]

    h = _ln_in(x, meta_tokens, ln_in_g, ln_in_b)
    for l in range(depth):
        h = _mix_layer(h, w_in[l], lb_all[l], hg_norm_g[l], w_hg_out[l], conv_w[l], conv_b[l],
                       conv_ln_g[l], conv_ln_b[l], w_conv_out[l], w_out[l], ln1_g[l], ln1_b[l],
                       alpha=alpha)
        jl = l // 2
        last = l == depth - 1
        if l % 2 == 0:
            h = _dense_ffn(h, ffn_w_gate[jl], ffn_w_up[jl], ffn_w_down[jl], ln2_g[l], ln2_b[l],
                           alpha=alpha)
            if last:
                h = h[:, TL:]
        else:
            assert last, "MoE layers other than the last one would need the padded layout back"
            h = _moe_layer(h, moe_router[jl], moe_w_gate[jl], moe_w_up[jl], moe_w_down[jl],
                           ln2_g[l], ln2_b[l], S, alpha=alpha)
    return h
```

```python
import functools

import jax
import jax.numpy as jnp
from jax import lax
from jax.experimental import pallas as pl
from jax.experimental.pallas import tpu as pltpu

F32 = jnp.float32
BF16 = jnp.bfloat16

N_META = 16
HG_HEADS = 8
HG_DK = 128
CONV_WIDTH = 31
N_EXPERTS = 8
LN_EPS = 1e-5
RMS_EPS = 1e-6

LANES = 128
SUBLANES = 8
TL = 256
HIST = 32
CONV_ROWS = 64
VMEM_LIMIT = 56 * 1024 * 1024


def _sigmoid(x):
    return 0.5 * jnp.tanh(0.5 * x) + 0.5


def _layer_norm(x, g, b):
    mu = jnp.mean(x, axis=-1, keepdims=True)
    xc = x - mu
    var = jnp.mean(xc * xc, axis=-1, keepdims=True)
    return xc * lax.rsqrt(var + LN_EPS) * g + b


def _dot(a, b):
    return jnp.dot(a, b, preferred_element_type=F32)


def _dot_nt(a, b):
    return lax.dot_general(a, b, (((1,), (1,)), ((), ())), preferred_element_type=F32)


def _dot_tn(a, b):
    return lax.dot_general(a, b, (((0,), (0,)), ((), ())), preferred_element_type=F32)


def _const_spec(shape):
    nd = len(shape)
    return pl.BlockSpec(shape, lambda *_: (0,) * nd, pipeline_mode=pl.Buffered(1))


def _ln_in_kernel(x_ref, meta_ref, g_ref, b_ref, out_ref, *, pad):
    j = pl.program_id(1)

    @pl.when(j == 0)
    def _():
        rows = lax.broadcasted_iota(jnp.int32, (TL, 1), 0)
        y = _layer_norm(meta_ref[...], g_ref[...], b_ref[...])
        out_ref[...] = jnp.where(rows >= pad, y, 0.0)

    @pl.when(j > 0)
    def _():
        out_ref[...] = _layer_norm(x_ref[...], g_ref[...], b_ref[...])


def _ln_in(x, meta_tokens, g, b):
    B, S, D = x.shape
    nt = 1 + S // TL
    pad = TL - N_META
    meta_pad = jnp.pad(meta_tokens.astype(F32), ((pad, 0), (0, 0)))
    return pl.pallas_call(
        functools.partial(_ln_in_kernel, pad=pad),
        grid=(B, nt),
        in_specs=[
            pl.BlockSpec((None, TL, D), lambda bi, j: (bi, jnp.maximum(j - 1, 0), 0)),
            _const_spec((TL, D)),
            _const_spec((1, D)),
            _const_spec((1, D)),
        ],
        out_specs=pl.BlockSpec((None, TL, D), lambda bi, j: (bi, j, 0)),
        out_shape=jax.ShapeDtypeStruct((B, nt * TL, D), F32),
        compiler_params=pltpu.CompilerParams(
            dimension_semantics=("arbitrary", "arbitrary")),
        name="ln_in",
    )(x, meta_pad, g.reshape(1, D), b.reshape(1, D))


def _cumsum_rows(x):
    n = x.shape[0]
    rows = lax.broadcasted_iota(jnp.int32, (n, 1), 0)
    s = 1
    while s < n:
        x = x + jnp.where(rows >= s, pltpu.roll(x, s, 0), 0.0)
        s *= 2
    return x


def _level_scores(qh, kh, bh, m):
    n = qh.shape[0]
    zero = jnp.zeros((m, qh.shape[1]), F32)
    qp, kp = [], []
    for jb in range(n // (2 * m)):
        lo, mid, hi = jb * 2 * m, jb * 2 * m + m, (jb + 1) * 2 * m
        e = bh[mid - 1:mid, :]
        kp += [kh[lo:mid] * jnp.exp(e - bh[lo:mid]), zero]
        qp += [zero, qh[mid:hi] * jnp.exp(bh[mid:hi] - e)]
    qp = jnp.concatenate(qp, axis=0).astype(BF16)
    kp = jnp.concatenate(kp, axis=0).astype(BF16)
    return _dot_nt(qp, kp)


def _mix_kernel(h_ref, w_in_ref, lb_ref, hgg_ref, w_hg_ref, cw_ref, cb_ref, clg_ref, clb_ref,
                w_cv_ref, w_out_ref, ln_g_ref, ln_b_ref, out_ref, outb_ref,
                st_s, hist_s, hb_s, q_s, k_s, b_s, v_s, og_s, ob_s, sh_s,
                *, pad, alpha):
    D = h_ref.shape[-1]
    H = HG_HEADS
    HALF = TL // 2
    j = pl.program_id(1)

    @pl.when(j == 0)
    def _():
        st_s[...] = jnp.zeros_like(st_s)
        hist_s[0:HIST, :] = jnp.zeros((HIST, D), F32)

    rows = j * TL + lax.broadcasted_iota(jnp.int32, (TL, 1), 0)
    hv = jnp.where(rows >= pad, h_ref[...], 0.0)
    hb_s[...] = hv.astype(BF16)

    def proj(g):
        return _dot(hb_s[...], w_in_ref[:, g * D:(g + 1) * D])

    def split_heads(dst, val):
        for hd in range(H):
            dst[hd] = val[:, hd * LANES:(hd + 1) * LANES]

    p = proj(0)
    split_heads(q_s, p * _sigmoid(p))

    z = proj(1)
    lb = lb_ref[...]
    e = jnp.exp(-jnp.abs(z))
    r = 1.0 / (1.0 + e)
    er = e * r
    pos = z >= 0.0
    sz = jnp.where(pos, r, er)
    snz = jnp.where(pos, er, r)
    log_sz = jnp.minimum(z, 0.0) - jnp.log(1.0 + e)
    log_f = jnp.where(lb > 0.0, jnp.log(lb + (1.0 - lb) * sz), log_sz)
    split_heads(k_s, (1.0 - lb) * snz)
    split_heads(b_s, _cumsum_rows(log_f))

    split_heads(v_s, proj(2))
    p = proj(3)
    split_heads(og_s, p * _sigmoid(p))
    hist_s[HIST:HIST + TL, :] = proj(4) * _sigmoid(proj(5))

    ri = lax.broadcasted_iota(jnp.int32, (HALF, HALF), 0)
    ci = lax.broadcasted_iota(jnp.int32, (HALF, HALF), 1)
    lane8 = lax.broadcasted_iota(jnp.int32, (SUBLANES, LANES), 1)
    sub8 = lax.broadcasted_iota(jnp.int32, (SUBLANES, LANES), 0)

    def head_body(hd, carry):
        q = q_s[hd]
        k = k_s[hd]
        bc = b_s[hd]
        vb = v_s[hd].astype(BF16)
        st = st_s[hd]
        b_last = bc[TL - 1:TL, :]

        o_inter = _dot_nt((q * jnp.exp(bc)).astype(BF16), st.astype(BF16))
        k_end = (k * jnp.exp(b_last - bc)).astype(BF16)
        st_s[hd] = st * jnp.exp(b_last) + _dot_tn(vb, k_end)

        def diag_block(r0):
            q8, k8, b8 = q[r0:r0 + SUBLANES], k[r0:r0 + SUBLANES], bc[r0:r0 + SUBLANES]
            c0 = r0 % HALF
            acc = jnp.zeros((SUBLANES, LANES), F32)
            for s in range(SUBLANES):
                dec = jnp.exp(b8 - b8[s:s + 1, :])
                col = jnp.sum(q8 * dec * k8[s:s + 1, :], axis=-1, keepdims=True)
                acc = jnp.where(lane8 == c0 + s, col, acc)
            keep = (lane8 >= c0) & (lane8 - c0 <= sub8)
            return jnp.where(keep, acc, 0.0)

        b_mid = bc[HALF - 1:HALF, :]
        a10 = _dot_nt((q[HALF:] * jnp.exp(bc[HALF:] - b_mid)).astype(BF16),
                      (k[:HALF] * jnp.exp(b_mid - bc[:HALF])).astype(BF16))
        atts = []
        for hh in range(2):
            sl = slice(hh * HALF, (hh + 1) * HALF)
            qh, kh, bh = q[sl], k[sl], bc[sl]
            att = jnp.concatenate(
                [diag_block(hh * HALF + i * SUBLANES) for i in range(HALF // SUBLANES)], axis=0)
            m = HALF // 2
            while m >= SUBLANES:
                sib = (ri & -(2 * m)) == (ci & -(2 * m))
                att = att + jnp.where(sib, _level_scores(qh, kh, bh, m), 0.0)
                m //= 2
            atts.append(att)
        o0 = _dot(atts[0].astype(BF16), vb[:HALF])
        o1 = _dot(jnp.concatenate([a10, atts[1]], axis=1).astype(BF16), vb)
        o = o_inter + jnp.concatenate([o0, o1], axis=0)

        ms = jnp.mean(o * o, axis=-1, keepdims=True)
        ob_s[hd] = (o * lax.rsqrt(ms + RMS_EPS) * hgg_ref[hd] * og_s[hd]).astype(BF16)
        return carry

    lax.fori_loop(0, H, head_body, 0)

    y_rec = _dot(jnp.concatenate([ob_s[hd] for hd in range(H)], axis=1), w_hg_ref[...])

    base = HIST - (CONV_WIDTH - 1)
    nshift = HIST + TL - SUBLANES
    strips = []
    for c in range(D // LANES):
        cs = slice(c * LANES, (c + 1) * LANES)
        for r in range(1, SUBLANES):
            sh_s[r - 1] = hist_s[r:r + nshift, cs]
        chunks = []
        for rc in range(0, TL, CONV_ROWS):
            acc = jnp.broadcast_to(cb_ref[:, cs], (CONV_ROWS, LANES))
            for t in range(CONV_WIDTH):
                a, r = divmod(base + t, SUBLANES)
                lo = a * SUBLANES + rc
                src = hist_s[lo:lo + CONV_ROWS, cs] if r == 0 else sh_s[r - 1, lo:lo + CONV_ROWS, :]
                acc = acc + src * cw_ref[t:t + 1, cs]
            chunks.append(acc)
        strips.append(jnp.concatenate(chunks, axis=0))
    yc = jnp.concatenate(strips, axis=1)
    hist_s[0:HIST, :] = hist_s[TL:TL + HIST, :]
    yc = _layer_norm(yc, clg_ref[...], clb_ref[...])
    y_conv = _dot((yc * _sigmoid(yc)).astype(BF16), w_cv_ref[...])

    mixed = _sigmoid(proj(6)) * y_rec + _sigmoid(proj(7)) * y_conv
    res = alpha * hv + _dot(mixed.astype(BF16), w_out_ref[...])
    hn = _layer_norm(res, ln_g_ref[...], ln_b_ref[...])
    out_ref[...] = hn
    outb_ref[...] = hn.astype(BF16)


def _mix_layer(h, w_in, lb, hg_g, w_hg, conv_w, conv_b, cln_g, cln_b, w_cv, w_out, ln_g, ln_b,
               *, alpha):
    B, Lp, D = h.shape
    H = HG_HEADS
    nt = Lp // TL
    row = lambda a: a.reshape(1, -1).astype(F32)
    return pl.pallas_call(
        functools.partial(_mix_kernel, pad=TL - N_META, alpha=alpha),
        grid=(B, nt),
        in_specs=[
            pl.BlockSpec((None, TL, D), lambda bi, j: (bi, j, 0)),
            _const_spec(w_in.shape),
            _const_spec((1, D)),
            _const_spec((H, 1, LANES)),
            _const_spec((D, D)),
            _const_spec((CONV_WIDTH, D)),
            _const_spec((1, D)),
            _const_spec((1, D)),
            _const_spec((1, D)),
            _const_spec((D, D)),
            _const_spec((D, D)),
            _const_spec((1, D)),
            _const_spec((1, D)),
        ],
        out_specs=[pl.BlockSpec((None, TL, D), lambda bi, j: (bi, j, 0)),
                   pl.BlockSpec((None, TL, D), lambda bi, j: (bi, j, 0))],
        out_shape=[jax.ShapeDtypeStruct((B, Lp, D), F32),
                   jax.ShapeDtypeStruct((B, Lp, D), BF16)],
        scratch_shapes=[
            pltpu.VMEM((H, HG_DK, HG_DK), F32),
            pltpu.VMEM((HIST + TL, D), F32),
            pltpu.VMEM((TL, D), BF16),
            pltpu.VMEM((H, TL, LANES), F32),
            pltpu.VMEM((H, TL, LANES), F32),
            pltpu.VMEM((H, TL, LANES), F32),
            pltpu.VMEM((H, TL, LANES), F32),
            pltpu.VMEM((H, TL, LANES), F32),
            pltpu.VMEM((H, TL, LANES), BF16),
            pltpu.VMEM((SUBLANES - 1, HIST + TL - SUBLANES, LANES), F32),
        ],
        compiler_params=pltpu.CompilerParams(
            dimension_semantics=("arbitrary", "arbitrary"),
            vmem_limit_bytes=VMEM_LIMIT),
        name="mix_layer",
    )(h, w_in.astype(BF16), row(lb), hg_g.reshape(H, 1, LANES).astype(F32), w_hg.astype(BF16),
      conv_w.astype(F32), row(conv_b), row(cln_g), row(cln_b), w_cv.astype(BF16),
      w_out.astype(BF16), row(ln_g), row(ln_b))


def _ffn_kernel(h_ref, hb_ref, wg_ref, wu_ref, wd_ref, g_ref, b_ref, out_ref, a_s, *, alpha, fc):
    xb = hb_ref[...]
    F = wg_ref.shape[1]
    for c in range(F // fc):
        cs = slice(c * fc, (c + 1) * fc)
        g = _dot(xb, wg_ref[:, cs])
        u = _dot(xb, wu_ref[:, cs])
        a_s[:, cs] = (g * _sigmoid(g) * u).astype(BF16)
    res = alpha * h_ref[...] + _dot(a_s[...], wd_ref[...])
    out_ref[...] = _layer_norm(res, g_ref[...], b_ref[...])


def _ffn_chunk(F):
    for fc in (512, 1408, 1024, 256, 128):
        if F % fc == 0:
            return fc
    return F


def _dense_ffn(h, hb, wg, wu, wd, ln_g, ln_b, *, alpha):
    B, Lp, D = h.shape
    F = wg.shape[1]
    R = B * Lp
    out = pl.pallas_call(
        functools.partial(_ffn_kernel, alpha=alpha, fc=_ffn_chunk(F)),
        grid=(R // TL,),
        in_specs=[
            pl.BlockSpec((TL, D), lambda i: (i, 0)),
            pl.BlockSpec((TL, D), lambda i: (i, 0)),
            _const_spec((D, F)),
            _const_spec((D, F)),
            _const_spec((F, D)),
            _const_spec((1, D)),
            _const_spec((1, D)),
        ],
        out_specs=pl.BlockSpec((TL, D), lambda i: (i, 0)),
        out_shape=jax.ShapeDtypeStruct((R, D), F32),
        scratch_shapes=[pltpu.VMEM((TL, F), BF16)],
        compiler_params=pltpu.CompilerParams(
            dimension_semantics=("arbitrary",), vmem_limit_bytes=VMEM_LIMIT),
        name="dense_ffn",
    )(h.reshape(R, D), hb.reshape(R, D), wg.astype(BF16), wu.astype(BF16), wd.astype(BF16),
      ln_g.reshape(1, D).astype(F32), ln_b.reshape(1, D).astype(F32))
    return out.reshape(B, Lp, D)


def _router_kernel(h_ref, wr_ref, gate_ref, rank_ref, cnt_s):
    first = (pl.program_id(0) == 0) & (pl.program_id(1) == 0)

    @pl.when(first)
    def _():
        cnt_s[...] = jnp.zeros_like(cnt_s)

    logits = jnp.dot(h_ref[...], wr_ref[...], preferred_element_type=F32,
                     precision=lax.Precision.HIGHEST)
    lane = lax.broadcasted_iota(jnp.int32, (TL, LANES), 1).astype(F32)
    neg = jnp.float32(-jnp.inf)
    logits = jnp.where(lane < N_EXPERTS, logits, neg)
    m1 = jnp.max(logits, axis=-1, keepdims=True)
    i1 = jnp.min(jnp.where(logits == m1, lane, float(LANES)), axis=-1, keepdims=True)
    oh1 = lane == i1
    rest = jnp.where(oh1, neg, logits)
    m2 = jnp.max(rest, axis=-1, keepdims=True)
    i2 = jnp.min(jnp.where(rest == m2, lane, float(LANES)), axis=-1, keepdims=True)
    oh2 = lane == i2
    e = jnp.exp(m2 - m1)
    w1 = 1.0 / (1.0 + e)
    w2 = e * w1
    gates = jnp.where(oh1, w1, 0.0) + jnp.where(oh2, w2, 0.0)
    gate_ref[...] = gates
    sel = (gates > 0.0).astype(BF16)
    ri = lax.broadcasted_iota(jnp.int32, (TL, TL), 0)
    ci = lax.broadcasted_iota(jnp.int32, (TL, TL), 1)
    before = _dot((ci < ri).astype(BF16), sel)
    cnt = cnt_s[...]
    rank_ref[...] = (before + cnt).astype(jnp.int32)
    cnt_s[...] = cnt + before[TL - 1:TL, :] + sel[TL - 1:TL, :].astype(F32)


def _router(h, w_router, seq):
    B, Lp, D = h.shape
    nj = seq // TL
    T = B * seq
    wr = jnp.pad(w_router.astype(F32), ((0, 0), (0, LANES - N_EXPERTS)))
    return pl.pallas_call(
        _router_kernel,
        grid=(B, nj),
        in_specs=[
            pl.BlockSpec((None, TL, D), lambda bi, j: (bi, j + 1, 0)),
            _const_spec((D, LANES)),
        ],
        out_specs=[
            pl.BlockSpec((TL, LANES), lambda bi, j: (bi * nj + j, 0)),
            pl.BlockSpec((TL, LANES), lambda bi, j: (bi * nj + j, 0)),
        ],
        out_shape=[jax.ShapeDtypeStruct((T, LANES), F32),
                   jax.ShapeDtypeStruct((T, LANES), jnp.int32)],
        scratch_shapes=[pltpu.VMEM((1, LANES), F32)],
        compiler_params=pltpu.CompilerParams(
            dimension_semantics=("arbitrary", "arbitrary")),
        name="moe_router",
    )(h, wr)


def _pair_list(lo, hi, tstart_of_group, blk_of_group, n_max):
    nonempty = hi > lo
    t_lo = lo // TL
    t_hi = jnp.where(nonempty, (hi - 1) // TL, t_lo - 1)
    npair = t_hi - t_lo + 1
    end = jnp.cumsum(npair)
    total = end[-1]
    start = end - npair
    p = jnp.arange(n_max, dtype=jnp.int32)
    pc = jnp.minimum(p, total - 1)
    g = jnp.sum((end[None, :] <= pc[:, None]).astype(jnp.int32), axis=1)
    hit = g[:, None] == jnp.arange(lo.shape[0], dtype=jnp.int32)[None, :]
    pick = lambda a: jnp.sum(jnp.where(hit, a[None, :], 0), axis=1)
    tile = pick(tstart_of_group + t_lo - start) + pc
    blk = pick(blk_of_group)
    valid = (p < total).astype(jnp.int32)
    return tile.astype(jnp.int32), blk.astype(jnp.int32), valid


def _expert_up_kernel(tile_ref, blk_ref, valid_ref, first_ref, last_ref, texp_ref,
                      h_ref, slot_ref, gate_ref, wg_ref, wu_ref, a_ref, gs_ref, x_s, g_s, *, fc):
    p = pl.program_id(0)

    @pl.when(first_ref[p] == 1)
    def _():
        x_s[...] = jnp.zeros_like(x_s)
        g_s[...] = jnp.zeros_like(g_s)

    @pl.when(valid_ref[p] == 1)
    def _():
        tile = tile_ref[p]
        e = texp_ref[tile]
        srow = slot_ref[pl.ds(e, 1), :] - tile * TL
        grow = gate_ref[pl.ds(e, 1), :]
        ri = lax.broadcasted_iota(jnp.int32, (TL, TL), 0)
        hit = ri == srow
        x_s[...] += _dot(hit.astype(BF16), h_ref[...]).astype(BF16)
        g_s[...] += jnp.sum(jnp.where(hit, grow, 0.0), axis=-1, keepdims=True)

    @pl.when(last_ref[p] == 1)
    def _():
        xb = x_s[...]
        F = wg_ref.shape[1]
        for c in range(F // fc):
            cs = slice(c * fc, (c + 1) * fc)
            g = _dot(xb, wg_ref[:, cs])
            u = _dot(xb, wu_ref[:, cs])
            a_ref[:, cs] = (g * _sigmoid(g) * u).astype(BF16)
        gs_ref[...] = g_s[...]


def _expert_down_kernel(texp_ref, nused_ref, a_ref, gs_ref, wd_ref, y_ref):
    i = pl.program_id(0)

    @pl.when(i < nused_ref[0])
    def _():
        y_ref[...] = (_dot(a_ref[...], wd_ref[...]) * gs_ref[:, 0:1]).astype(BF16)


def _combine_kernel(cidx_ref, cval_ref, *refs, alpha):
    npair = 2 * N_EXPERTS
    y_refs = refs[:npair]
    slot_ref, h_ref, g_ref, b_ref, out_ref, acc_s = refs[npair:]
    b = pl.program_id(0)
    ci = lax.broadcasted_iota(jnp.int32, (TL, TL), 1)

    def contrib(k):
        e = k // 2
        scol = slot_ref[:, e:e + 1] - cidx_ref[b * npair + k] * TL
        return _dot((ci == scol).astype(BF16), y_refs[k][...])

    acc = contrib(0)
    for e in range(1, N_EXPERTS):
        acc = acc + contrib(2 * e)
    acc_s[...] = acc
    for e in range(N_EXPERTS):
        @pl.when(cval_ref[b * N_EXPERTS + e] == 1)
        def _(e=e):
            acc_s[...] += contrib(2 * e + 1)

    res = alpha * h_ref[...] + acc_s[...]
    out_ref[...] = _layer_norm(res, g_ref[...], b_ref[...])


def _moe_layer(h, hb, w_router, wg, wu, wd, ln_g, ln_b, seq, *, alpha):
    B, Lp, D = h.shape
    E = N_EXPERTS
    F = wg.shape[-1]
    nj = seq // TL
    T = B * seq
    NB = T // TL
    NT = (2 * T) // TL + E
    NP = NB * E + NT

    gate, rank = _router(h, w_router, seq)

    gate8 = gate[:, :E]
    rank8 = rank[:, :E]
    sel8 = gate8 > 0.0
    counts = rank8[-1] + sel8[-1].astype(jnp.int32)
    ntile = (counts + TL - 1) // TL
    tstart = jnp.cumsum(ntile) - ntile
    n_used = jnp.sum(ntile).astype(jnp.int32).reshape(1)
    slot8 = jnp.where(sel8, tstart[None, :] * TL + rank8, -1).astype(jnp.int32)
    slot_tm = jnp.pad(slot8, ((0, 0), (0, LANES - E)), constant_values=-1)
    slot_em = slot8.reshape(NB, TL, E).transpose(0, 2, 1)
    gate_em = gate8.reshape(NB, TL, E).transpose(0, 2, 1)
    tiles = jnp.arange(NT, dtype=jnp.int32)
    texp = jnp.sum((jnp.cumsum(ntile)[None, :] <= tiles[:, None]).astype(jnp.int32), axis=1)
    texp = jnp.minimum(texp, E - 1)

    cb = jnp.concatenate([rank8[::TL], counts[None, :]], axis=0)
    lo_be, hi_be = cb[:-1], cb[1:]
    blk_ids = jnp.arange(NB, dtype=jnp.int32)
    d_tile, d_blk, d_valid = _pair_list(
        lo_be.T.reshape(-1), hi_be.T.reshape(-1), jnp.repeat(tstart, NB),
        jnp.tile(blk_ids, E), NP)
    chg = (d_tile[1:] != d_tile[:-1]).astype(jnp.int32)
    one = jnp.ones((1,), jnp.int32)
    d_first = jnp.concatenate([one, chg]) * d_valid
    d_next_valid = jnp.concatenate([d_valid[1:], jnp.zeros((1,), jnp.int32)])
    d_last = jnp.where(d_next_valid == 0, 1, jnp.concatenate([chg, one])) * d_valid
    nonempty = hi_be > lo_be
    t0 = tstart[None, :] + lo_be // TL
    cross = nonempty & ((hi_be - 1) // TL > lo_be // TL)
    earlier = (blk_ids[None, :] <= blk_ids[:, None])[:, :, None]
    fill = lambda v: jnp.max(jnp.where(earlier, v[None, :, :], 0), axis=1)
    cidx = jnp.stack([fill(jnp.where(nonempty, t0, 0)), fill(jnp.where(cross, t0 + 1, 0))],
                     axis=-1).reshape(-1).astype(jnp.int32)
    cval = cross.reshape(-1).astype(jnp.int32)

    hblk = lambda blk: (blk // nj, blk % nj + 1, 0)
    npf = lambda fn: (lambda p, t, b, v, f, l, x: fn(p, t, b, x))

    act, gslot = pl.pallas_call(
        functools.partial(_expert_up_kernel, fc=_ffn_chunk(F)),
        grid_spec=pltpu.PrefetchScalarGridSpec(
            num_scalar_prefetch=6,
            grid=(NP,),
            in_specs=[
                pl.BlockSpec((None, TL, D), npf(lambda p, t, b, x: hblk(b[p]))),
                pl.BlockSpec((None, E, TL), npf(lambda p, t, b, x: (b[p], 0, 0))),
                pl.BlockSpec((None, E, TL), npf(lambda p, t, b, x: (b[p], 0, 0))),
                pl.BlockSpec((None, D, F), npf(lambda p, t, b, x: (x[t[p]], 0, 0))),
                pl.BlockSpec((None, D, F), npf(lambda p, t, b, x: (x[t[p]], 0, 0))),
            ],
            out_specs=[pl.BlockSpec((TL, F), npf(lambda p, t, b, x: (t[p], 0))),
                       pl.BlockSpec((TL, LANES), npf(lambda p, t, b, x: (t[p], 0)))],
            scratch_shapes=[pltpu.VMEM((TL, D), BF16), pltpu.VMEM((TL, LANES), F32)],
        ),
        out_shape=[jax.ShapeDtypeStruct((NT * TL, F), BF16),
                   jax.ShapeDtypeStruct((NT * TL, LANES), F32)],
        compiler_params=pltpu.CompilerParams(
            dimension_semantics=("arbitrary",), vmem_limit_bytes=VMEM_LIMIT),
        name="moe_expert_up",
    )(d_tile, d_blk, d_valid, d_first, d_last, texp, hb, slot_em, gate_em,
      wg.astype(BF16), wu.astype(BF16))

    used = lambda i, n: jnp.minimum(i, n[0] - 1)
    ys = pl.pallas_call(
        _expert_down_kernel,
        grid_spec=pltpu.PrefetchScalarGridSpec(
            num_scalar_prefetch=2,
            grid=(NT,),
            in_specs=[
                pl.BlockSpec((TL, F), lambda i, x, n: (used(i, n), 0)),
                pl.BlockSpec((TL, LANES), lambda i, x, n: (used(i, n), 0)),
                pl.BlockSpec((None, F, D), lambda i, x, n: (x[used(i, n)], 0, 0)),
            ],
            out_specs=pl.BlockSpec((TL, D), lambda i, x, n: (used(i, n), 0)),
        ),
        out_shape=jax.ShapeDtypeStruct((NT * TL, D), BF16),
        compiler_params=pltpu.CompilerParams(
            dimension_semantics=("arbitrary",), vmem_limit_bytes=VMEM_LIMIT),
        name="moe_expert_down",
    )(texp, n_used, act, gslot, wd.astype(BF16))

    ypair = lambda k: pl.BlockSpec((TL, D), lambda b, ci, cv: (ci[b * 2 * E + k], 0))
    out = pl.pallas_call(
        functools.partial(_combine_kernel, alpha=alpha),
        grid_spec=pltpu.PrefetchScalarGridSpec(
            num_scalar_prefetch=2,
            grid=(NB,),
            in_specs=[ypair(k) for k in range(2 * E)] + [
                pl.BlockSpec((TL, LANES), lambda b, ci, cv: (b, 0)),
                pl.BlockSpec((None, TL, D), lambda b, ci, cv: hblk(b)),
                pl.BlockSpec((1, D), lambda b, ci, cv: (0, 0)),
                pl.BlockSpec((1, D), lambda b, ci, cv: (0, 0)),
            ],
            out_specs=pl.BlockSpec((None, TL, D), lambda b, ci, cv: (b // nj, b % nj, 0)),
            scratch_shapes=[pltpu.VMEM((TL, D), F32)],
        ),
        out_shape=jax.ShapeDtypeStruct((B, seq, D), F32),
        compiler_params=pltpu.CompilerParams(
            dimension_semantics=("arbitrary",), vmem_limit_bytes=VMEM_LIMIT),
        name="moe_combine",
    )(cidx, cval, *([ys] * (2 * E)), slot_tm, h,
      ln_g.reshape(1, D).astype(F32), ln_b.reshape(1, D).astype(F32))
    return out


def kernel(x, meta_tokens, ln_in_g, ln_in_b, w_in, lower_bounds, hg_norm_g, w_hg_out, conv_w,
           conv_b, conv_ln_g, conv_ln_b, w_conv_out, w_out, ln1_g, ln1_b, ffn_w_gate, ffn_w_up,
           ffn_w_down, moe_router, moe_w_gate, moe_w_up, moe_w_down, ln2_g, ln2_b):
    B, S, D = x.shape
    depth = w_in.shape[0]
    assert S % TL == 0 and D % LANES == 0 and meta_tokens.shape[0] == N_META
    alpha = float((2 * depth) ** 0.25)

    pr = jax.nn.softmax(lower_bounds.astype(F32), axis=0)
    lb_all = jnp.cumsum(pr, axis=0) - pr[0]

    h = _ln_in(x, meta_tokens, ln_in_g, ln_in_b)
    for l in range(depth):
        h, hb = _mix_layer(h, w_in[l], lb_all[l], hg_norm_g[l], w_hg_out[l], conv_w[l], conv_b[l],
                           conv_ln_g[l], conv_ln_b[l], w_conv_out[l], w_out[l], ln1_g[l],
                           ln1_b[l], alpha=alpha)
        jl = l // 2
        last = l == depth - 1
        if l % 2 == 0:
            h = _dense_ffn(h, hb, ffn_w_gate[jl], ffn_w_up[jl], ffn_w_down[jl], ln2_g[l],
                           ln2_b[l], alpha=alpha)
            if last:
                h = h[:, TL:]
        else:
            assert last, "MoE layers other than the last one would need the padded layout back"
            h = _moe_layer(h, hb, moe_router[jl], moe_w_gate[jl], moe_w_up[jl], moe_w_down[jl],
                           ln2_g[l], ln2_b[l], S, alpha=alpha)
    return h
```

```python
import functools

import jax
import jax.numpy as jnp
from jax import lax
from jax.experimental import pallas as pl
from jax.experimental.pallas import tpu as pltpu

F32 = jnp.float32
BF16 = jnp.bfloat16

N_META = 16
HG_HEADS = 8
HG_DK = 128
CONV_WIDTH = 31
N_EXPERTS = 8
LN_EPS = 1e-5
RMS_EPS = 1e-6

LANES = 128
SUBLANES = 8
TL = 256
HIST = 32
CONV_ROWS = 64
GATHER_BLOCKS = 8
VMEM_LIMIT = 56 * 1024 * 1024
LOG2E = 1.4426950408889634


def _sigmoid(x):
    return 0.5 * jnp.tanh(0.5 * x) + 0.5


def _layer_norm(x, g, b):
    mu = jnp.mean(x, axis=-1, keepdims=True)
    xc = x - mu
    var = jnp.mean(xc * xc, axis=-1, keepdims=True)
    return xc * lax.rsqrt(var + LN_EPS) * g + b


def _dot(a, b):
    return jnp.dot(a, b, preferred_element_type=F32)


def _dot_nt(a, b):
    return lax.dot_general(a, b, (((1,), (1,)), ((), ())), preferred_element_type=F32)


def _dot_tn(a, b):
    return lax.dot_general(a, b, (((0,), (0,)), ((), ())), preferred_element_type=F32)


def _const_spec(shape):
    nd = len(shape)
    return pl.BlockSpec(shape, lambda *_: (0,) * nd, pipeline_mode=pl.Buffered(1))


def _ln_in_kernel(x_ref, meta_ref, g_ref, b_ref, out_ref, *, pad):
    j = pl.program_id(1)

    @pl.when(j == 0)
    def _():
        rows = lax.broadcasted_iota(jnp.int32, (TL, 1), 0)
        y = _layer_norm(meta_ref[...], g_ref[...], b_ref[...])
        out_ref[...] = jnp.where(rows >= pad, y, 0.0)

    @pl.when(j > 0)
    def _():
        out_ref[...] = _layer_norm(x_ref[...], g_ref[...], b_ref[...])


def _ln_in(x, meta_tokens, g, b):
    B, S, D = x.shape
    nt = 1 + S // TL
    pad = TL - N_META
    meta_pad = jnp.pad(meta_tokens.astype(F32), ((pad, 0), (0, 0)))
    return pl.pallas_call(
        functools.partial(_ln_in_kernel, pad=pad),
        grid=(B, nt),
        in_specs=[
            pl.BlockSpec((None, TL, D), lambda bi, j: (bi, jnp.maximum(j - 1, 0), 0)),
            _const_spec((TL, D)),
            _const_spec((1, D)),
            _const_spec((1, D)),
        ],
        out_specs=pl.BlockSpec((None, TL, D), lambda bi, j: (bi, j, 0)),
        out_shape=jax.ShapeDtypeStruct((B, nt * TL, D), F32),
        compiler_params=pltpu.CompilerParams(
            dimension_semantics=("arbitrary", "arbitrary")),
        name="ln_in",
    )(x, meta_pad, g.reshape(1, D), b.reshape(1, D))


def _cumsum_rows(x):
    n = x.shape[0]
    hi = x.astype(BF16)
    r1 = x - hi.astype(F32)
    mid = r1.astype(BF16)
    lo = (r1 - mid.astype(F32)).astype(BF16)
    ri = lax.broadcasted_iota(jnp.int32, (n, n), 0)
    ci = lax.broadcasted_iota(jnp.int32, (n, n), 1)
    tri = (ri >= ci).astype(BF16)
    return _dot(tri, hi) + _dot(tri, mid) + _dot(tri, lo)


def _level_scores(qh, ch, bh, m):
    n = qh.shape[0]
    zero = jnp.zeros((m, qh.shape[1]), F32)
    qp, kp = [], []
    for jb in range(n // (2 * m)):
        lo, mid, hi = jb * 2 * m, jb * 2 * m + m, (jb + 1) * 2 * m
        e = bh[mid - 1:mid, :]
        kp += [jnp.exp2(e + ch[lo:mid]), zero]
        qp += [zero, qh[mid:hi] * jnp.exp2(bh[mid:hi] - e)]
    qp = jnp.concatenate(qp, axis=0).astype(BF16)
    kp = jnp.concatenate(kp, axis=0).astype(BF16)
    return _dot_nt(qp, kp)


def _mix_kernel(h_ref, w_in_ref, lb_ref, hgg_ref, w_hg_ref, cw_ref, cb_ref, clg_ref, clb_ref,
                w_cv_ref, w_out_ref, ln_g_ref, ln_b_ref, out_ref, outb_ref,
                st_s, hist_s, hb_s, q_s, c_s, b_s, v_s, og_s, ob_s, sh_s,
                *, pad, alpha):
    D = h_ref.shape[-1]
    H = HG_HEADS
    HALF = TL // 2
    j = pl.program_id(1)

    @pl.when(j == 0)
    def _():
        st_s[...] = jnp.zeros_like(st_s)
        hist_s[0:HIST, :] = jnp.zeros((HIST, D), F32)

    rows = j * TL + lax.broadcasted_iota(jnp.int32, (TL, 1), 0)
    hv = jnp.where(rows >= pad, h_ref[...], 0.0)
    hb_s[...] = hv.astype(BF16)

    def proj(g):
        return _dot(hb_s[...], w_in_ref[:, g * D:(g + 1) * D])

    def split_heads(dst, val):
        for hd in range(H):
            dst[hd] = val[:, hd * LANES:(hd + 1) * LANES]


    p = proj(0)
    split_heads(q_s, p * (jnp.tanh(p) + 1.0))

    z = proj(1)
    lb = lb_ref[...]
    one_m_lb = 1.0 - lb
    e = jnp.exp2(jnp.abs(z) * (-LOG2E))
    log_sz = jnp.minimum(z, 0.0) - jnp.log(1.0 + e)
    f = lb + one_m_lb * jnp.exp(log_sz)
    log_f = jnp.where(lb > 0.0, jnp.log(f), log_sz)
    log_k = (log_sz - z) + jnp.log(one_m_lb)
    bq = _cumsum_rows(log_f) * LOG2E
    split_heads(b_s, bq)
    split_heads(c_s, log_k * LOG2E - bq)

    split_heads(v_s, proj(2))
    p = proj(3)
    split_heads(og_s, p * (jnp.tanh(p) + 1.0))
    hist_s[HIST:HIST + TL, :] = proj(4) * (jnp.tanh(proj(5)) + 1.0)

    ri = lax.broadcasted_iota(jnp.int32, (HALF, HALF), 0)
    ci = lax.broadcasted_iota(jnp.int32, (HALF, HALF), 1)
    lane8 = lax.broadcasted_iota(jnp.int32, (SUBLANES, LANES), 1)
    sub8 = lax.broadcasted_iota(jnp.int32, (SUBLANES, LANES), 0)

    def head_body(hd, carry):
        q = q_s[hd]
        bc = b_s[hd]
        cc = c_s[hd]
        vb = v_s[hd].astype(BF16)
        st = st_s[hd]
        b_last = bc[TL - 1:TL, :]

        o_inter = _dot_nt((q * jnp.exp2(bc)).astype(BF16), st.astype(BF16))
        k_end = jnp.exp2(b_last + cc).astype(BF16)
        st_s[hd] = st * jnp.exp2(b_last) + _dot_tn(vb, k_end)

        def diag_block(r0):
            q8, b8 = q[r0:r0 + SUBLANES], bc[r0:r0 + SUBLANES]
            c0 = r0 % HALF
            acc = jnp.zeros((SUBLANES, LANES), F32)
            for s in range(SUBLANES):
                w = jnp.exp2(b8 + c_s[hd, r0 + s:r0 + s + 1, :])
                col = jnp.sum(q8 * w, axis=-1, keepdims=True)
                acc = jnp.where(lane8 == c0 + s, col, acc)
            keep = (lane8 >= c0) & (lane8 - c0 <= sub8)
            return jnp.where(keep, acc, 0.0)

        b_mid = bc[HALF - 1:HALF, :]
        a10 = _dot_nt((q[HALF:] * jnp.exp2(bc[HALF:] - b_mid)).astype(BF16),
                      jnp.exp2(b_mid + cc[:HALF]).astype(BF16))
        atts = []
        for hh in range(2):
            sl = slice(hh * HALF, (hh + 1) * HALF)
            qh, ch, bh = q[sl], cc[sl], bc[sl]
            att = jnp.concatenate(
                [diag_block(hh * HALF + i * SUBLANES) for i in range(HALF // SUBLANES)], axis=0)
            m = HALF // 2
            while m >= SUBLANES:
                sib = (ri & -(2 * m)) == (ci & -(2 * m))
                att = att + jnp.where(sib, _level_scores(qh, ch, bh, m), 0.0)
                m //= 2
            atts.append(att)
        o0 = _dot(atts[0].astype(BF16), vb[:HALF])
        o1 = _dot(jnp.concatenate([a10, atts[1]], axis=1).astype(BF16), vb)
        o = o_inter + jnp.concatenate([o0, o1], axis=0)

        ms = jnp.mean(o * o, axis=-1, keepdims=True)
        ob_s[hd] = (o * lax.rsqrt(ms + RMS_EPS) * hgg_ref[hd] * og_s[hd]).astype(BF16)
        return carry

    for hd in range(H):
        head_body(hd, 0)

    y_rec = _dot(jnp.concatenate([ob_s[hd] for hd in range(H)], axis=1), w_hg_ref[...])

    base = HIST - (CONV_WIDTH - 1)
    nshift = HIST + TL - SUBLANES
    strips = []
    for c in range(D // LANES):
        cs = slice(c * LANES, (c + 1) * LANES)
        for r in range(1, SUBLANES):
            sh_s[r - 1] = hist_s[r:r + nshift, cs]
        chunks = []
        for rc in range(0, TL, CONV_ROWS):
            acc = jnp.broadcast_to(cb_ref[:, cs], (CONV_ROWS, LANES))
            for t in range(CONV_WIDTH):
                a, r = divmod(base + t, SUBLANES)
                lo = a * SUBLANES + rc
                src = hist_s[lo:lo + CONV_ROWS, cs] if r == 0 else sh_s[r - 1, lo:lo + CONV_ROWS, :]
                acc = acc + src * cw_ref[t:t + 1, cs]
            chunks.append(acc)
        strips.append(jnp.concatenate(chunks, axis=0))
    yc = jnp.concatenate(strips, axis=1)
    hist_s[0:HIST, :] = hist_s[TL:TL + HIST, :]
    yc = _layer_norm(yc, clg_ref[...], clb_ref[...])
    y_conv = _dot((yc * (jnp.tanh(yc) + 1.0)).astype(BF16), w_cv_ref[...])

    mixed = (jnp.tanh(proj(6)) + 1.0) * y_rec + (jnp.tanh(proj(7)) + 1.0) * y_conv
    res = alpha * hv + _dot(mixed.astype(BF16), w_out_ref[...])
    hn = _layer_norm(res, ln_g_ref[...], ln_b_ref[...])
    out_ref[...] = hn
    outb_ref[...] = hn.astype(BF16)


def _mix_layer(h, w_in, lb, hg_g, w_hg, conv_w, conv_b, cln_g, cln_b, w_cv, w_out, ln_g, ln_b,
               *, alpha):
    B, Lp, D = h.shape
    H = HG_HEADS
    nt = Lp // TL
    row = lambda a: a.reshape(1, -1).astype(F32)
    col_half = jnp.concatenate([
        jnp.full((D,), 0.5, F32), jnp.ones((2 * D,), F32), jnp.full((5 * D,), 0.5, F32)])
    w_in = w_in * col_half
    cln_g, cln_b, w_out = 0.5 * cln_g, 0.5 * cln_b, 0.5 * w_out
    return pl.pallas_call(
        functools.partial(_mix_kernel, pad=TL - N_META, alpha=alpha),
        grid=(B, nt),
        in_specs=[
            pl.BlockSpec((None, TL, D), lambda bi, j: (bi, j, 0)),
            _const_spec(w_in.shape),
            _const_spec((1, D)),
            _const_spec((H, 1, LANES)),
            _const_spec((D, D)),
            _const_spec((CONV_WIDTH, D)),
            _const_spec((1, D)),
            _const_spec((1, D)),
            _const_spec((1, D)),
            _const_spec((D, D)),
            _const_spec((D, D)),
            _const_spec((1, D)),
            _const_spec((1, D)),
        ],
        out_specs=[pl.BlockSpec((None, TL, D), lambda bi, j: (bi, j, 0)),
                   pl.BlockSpec((None, TL, D), lambda bi, j: (bi, j, 0))],
        out_shape=[jax.ShapeDtypeStruct((B, Lp, D), F32),
                   jax.ShapeDtypeStruct((B, Lp, D), BF16)],
        scratch_shapes=[
            pltpu.VMEM((H, HG_DK, HG_DK), F32),
            pltpu.VMEM((HIST + TL, D), F32),
            pltpu.VMEM((TL, D), BF16),
            pltpu.VMEM((H, TL, LANES), F32),
            pltpu.VMEM((H, TL, LANES), F32),
            pltpu.VMEM((H, TL, LANES), F32),
            pltpu.VMEM((H, TL, LANES), F32),
            pltpu.VMEM((H, TL, LANES), F32),
            pltpu.VMEM((H, TL, LANES), BF16),
            pltpu.VMEM((SUBLANES - 1, HIST + TL - SUBLANES, LANES), F32),
        ],
        compiler_params=pltpu.CompilerParams(
            dimension_semantics=("arbitrary", "arbitrary"),
            vmem_limit_bytes=VMEM_LIMIT),
        name="mix_layer",
    )(h, w_in.astype(BF16), row(lb), hg_g.reshape(H, 1, LANES).astype(F32), w_hg.astype(BF16),
      conv_w.astype(F32), row(conv_b), row(cln_g), row(cln_b), w_cv.astype(BF16),
      w_out.astype(BF16), row(ln_g), row(ln_b))


def _ffn_kernel(h_ref, hb_ref, wg_ref, wu_ref, wd_ref, g_ref, b_ref, out_ref, a_s, *, alpha, fc):
    xb = hb_ref[...]
    F = wg_ref.shape[1]
    for c in range(F // fc):
        cs = slice(c * fc, (c + 1) * fc)
        g = _dot(xb, wg_ref[:, cs])
        u = _dot(xb, wu_ref[:, cs])
        a_s[:, cs] = (g * _sigmoid(g) * u).astype(BF16)
    res = alpha * h_ref[...] + _dot(a_s[...], wd_ref[...])
    out_ref[...] = _layer_norm(res, g_ref[...], b_ref[...])


def _ffn_chunk(F):
    for fc in (512, 1408, 1024, 256, 128):
        if F % fc == 0:
            return fc
    return F


def _dense_ffn(h, hb, wg, wu, wd, ln_g, ln_b, *, alpha):
    B, Lp, D = h.shape
    F = wg.shape[1]
    R = B * Lp
    out = pl.pallas_call(
        functools.partial(_ffn_kernel, alpha=alpha, fc=_ffn_chunk(F)),
        grid=(R // TL,),
        in_specs=[
            pl.BlockSpec((TL, D), lambda i: (i, 0)),
            pl.BlockSpec((TL, D), lambda i: (i, 0)),
            _const_spec((D, F)),
            _const_spec((D, F)),
            _const_spec((F, D)),
            _const_spec((1, D)),
            _const_spec((1, D)),
        ],
        out_specs=pl.BlockSpec((TL, D), lambda i: (i, 0)),
        out_shape=jax.ShapeDtypeStruct((R, D), F32),
        scratch_shapes=[pltpu.VMEM((TL, F), BF16)],
        compiler_params=pltpu.CompilerParams(
            dimension_semantics=("arbitrary",), vmem_limit_bytes=VMEM_LIMIT),
        name="dense_ffn",
    )(h.reshape(R, D), hb.reshape(R, D), wg.astype(BF16), wu.astype(BF16), wd.astype(BF16),
      ln_g.reshape(1, D).astype(F32), ln_b.reshape(1, D).astype(F32))
    return out.reshape(B, Lp, D)


def _router_kernel(h_ref, wr_ref, gate_ref, rank_ref, cnt_s):
    first = (pl.program_id(0) == 0) & (pl.program_id(1) == 0)

    @pl.when(first)
    def _():
        cnt_s[...] = jnp.zeros_like(cnt_s)

    hv, wv = h_ref[...], wr_ref[...]
    h_hi, w_hi = hv.astype(BF16), wv.astype(BF16)
    h_lo = (hv - h_hi.astype(F32)).astype(BF16)
    w_lo = (wv - w_hi.astype(F32)).astype(BF16)
    logits = _dot(h_hi, w_hi) + (_dot(h_hi, w_lo) + _dot(h_lo, w_hi))
    lane = lax.broadcasted_iota(jnp.int32, (TL, LANES), 1).astype(F32)
    neg = jnp.float32(-jnp.inf)
    logits = jnp.where(lane < N_EXPERTS, logits, neg)
    m1 = jnp.max(logits, axis=-1, keepdims=True)
    i1 = jnp.min(jnp.where(logits == m1, lane, float(LANES)), axis=-1, keepdims=True)
    oh1 = lane == i1
    rest = jnp.where(oh1, neg, logits)
    m2 = jnp.max(rest, axis=-1, keepdims=True)
    i2 = jnp.min(jnp.where(rest == m2, lane, float(LANES)), axis=-1, keepdims=True)
    oh2 = lane == i2
    e = jnp.exp(m2 - m1)
    w1 = 1.0 / (1.0 + e)
    w2 = e * w1
    gates = jnp.where(oh1, w1, 0.0) + jnp.where(oh2, w2, 0.0)
    gate_ref[...] = gates
    sel = (gates > 0.0).astype(BF16)
    ri = lax.broadcasted_iota(jnp.int32, (TL, TL), 0)
    ci = lax.broadcasted_iota(jnp.int32, (TL, TL), 1)
    before = _dot((ci < ri).astype(BF16), sel)
    cnt = cnt_s[...]
    rank_ref[...] = (before + cnt).astype(jnp.int32)
    cnt_s[...] = cnt + before[TL - 1:TL, :] + sel[TL - 1:TL, :].astype(F32)


def _router(h, w_router, seq):
    B, Lp, D = h.shape
    nj = seq // TL
    T = B * seq
    wr = jnp.pad(w_router.astype(F32), ((0, 0), (0, LANES - N_EXPERTS)))
    return pl.pallas_call(
        _router_kernel,
        grid=(B, nj),
        in_specs=[
            pl.BlockSpec((None, TL, D), lambda bi, j: (bi, j + 1, 0)),
            _const_spec((D, LANES)),
        ],
        out_specs=[
            pl.BlockSpec((TL, LANES), lambda bi, j: (bi * nj + j, 0)),
            pl.BlockSpec((TL, LANES), lambda bi, j: (bi * nj + j, 0)),
        ],
        out_shape=[jax.ShapeDtypeStruct((T, LANES), F32),
                   jax.ShapeDtypeStruct((T, LANES), jnp.int32)],
        scratch_shapes=[pltpu.VMEM((1, LANES), F32)],
        compiler_params=pltpu.CompilerParams(
            dimension_semantics=("arbitrary", "arbitrary")),
        name="moe_router",
    )(h, wr)


def _gather_steps(lo_be, hi_be, counts, tstart, texp, n_used, n_steps):
    nb, ne = lo_be.shape
    nt = texp.shape[0]
    tiles = jnp.arange(nt, dtype=jnp.int32)
    of_tile = texp[:, None] == jnp.arange(ne, dtype=jnp.int32)[None, :]
    pick_e = lambda a: jnp.sum(jnp.where(of_tile[:, :, None], a.T[None, :, :], 0), axis=1)
    pick_v = lambda v: jnp.sum(jnp.where(of_tile, v[None, :], 0), axis=1)
    s0 = (tiles - pick_v(tstart)) * TL
    s1 = jnp.minimum(s0 + TL, pick_v(counts))
    ov = (pick_e(lo_be) < s1[:, None]) & (pick_e(hi_be) > s0[:, None]) & (tiles < n_used)[:, None]
    nblk = jnp.sum(ov.astype(jnp.int32), axis=1)
    blk_ids = jnp.arange(nb, dtype=jnp.int32)
    b_first = jnp.min(jnp.where(ov, blk_ids[None, :], nb), axis=1)
    nstep = (nblk + GATHER_BLOCKS - 1) // GATHER_BLOCKS
    end = jnp.cumsum(nstep)
    total = end[-1]
    start = end - nstep
    s = jnp.arange(n_steps, dtype=jnp.int32)
    sc = jnp.minimum(s, total - 1)
    tile = jnp.sum((end[None, :] <= sc[:, None]).astype(jnp.int32), axis=1)
    hit = tile[:, None] == tiles[None, :]
    pick_t = lambda a: jnp.sum(jnp.where(hit, a[None, :], 0), axis=1)
    u = sc - pick_t(start)
    valid = s < total
    b0 = pick_t(b_first) + GATHER_BLOCKS * u
    b_last = pick_t(b_first + nblk - 1)
    nv = jnp.where(valid, jnp.minimum(GATHER_BLOCKS, pick_t(nblk) - GATHER_BLOCKS * u), 0)
    first = valid & (u == 0)
    last = valid & (u == pick_t(nstep) - 1)
    i32 = lambda a: a.astype(jnp.int32)
    return i32(tile), i32(b0), i32(b_last), i32(nv), i32(first), i32(last)


def _expert_up_kernel(tile_ref, b0_ref, bl_ref, nv_ref, first_ref, last_ref, texp_ref, *refs, fc):
    nk = GATHER_BLOCKS
    h_refs, slot_refs, gate_refs = refs[:nk], refs[nk:2 * nk], refs[2 * nk:3 * nk]
    wg_ref, wu_ref, a_ref, gs_ref, x_s, g_s = refs[3 * nk:]
    p = pl.program_id(0)
    tile = tile_ref[p]
    e = texp_ref[tile]

    @pl.when(first_ref[p] == 1)
    def _():
        x_s[...] = jnp.zeros_like(x_s)
        g_s[...] = jnp.zeros_like(g_s)

    for k in range(nk):
        @pl.when(k < nv_ref[p])
        def _(k=k):
            srow = slot_refs[k][pl.ds(e, 1), :] - tile * TL
            grow = gate_refs[k][pl.ds(e, 1), :]
            ri = lax.broadcasted_iota(jnp.int32, (TL, TL), 0)
            hit = ri == srow
            x_s[...] += _dot(hit.astype(BF16), h_refs[k][...]).astype(BF16)
            g_s[...] += jnp.sum(jnp.where(hit, grow, 0.0), axis=-1, keepdims=True)

    @pl.when(last_ref[p] == 1)
    def _():
        xb = x_s[...]
        F = wg_ref.shape[1]
        for c in range(F // fc):
            cs = slice(c * fc, (c + 1) * fc)
            g = _dot(xb, wg_ref[:, cs])
            u = _dot(xb, wu_ref[:, cs])
            a_ref[:, cs] = (g * _sigmoid(g) * u).astype(BF16)
        gs_ref[...] = g_s[...]


def _expert_down_kernel(texp_ref, nused_ref, a_ref, gs_ref, wd_ref, y_ref):
    i = pl.program_id(0)

    @pl.when(i < nused_ref[0])
    def _():
        y_ref[...] = (_dot(a_ref[...], wd_ref[...]) * gs_ref[:, 0:1]).astype(BF16)


def _combine_kernel(cidx_ref, cval_ref, *refs, alpha):
    npair = 2 * N_EXPERTS
    y_refs = refs[:npair]
    slot_ref, h_ref, g_ref, b_ref, out_ref, acc_s = refs[npair:]
    b = pl.program_id(0)
    ci = lax.broadcasted_iota(jnp.int32, (TL, TL), 1)

    def contrib(k):
        e = k // 2
        scol = slot_ref[:, e:e + 1] - cidx_ref[b * npair + k] * TL
        return _dot((ci == scol).astype(BF16), y_refs[k][...])

    acc = contrib(0)
    for e in range(1, N_EXPERTS):
        acc = acc + contrib(2 * e)
    acc_s[...] = acc
    for e in range(N_EXPERTS):
        @pl.when(cval_ref[b * N_EXPERTS + e] == 1)
        def _(e=e):
            acc_s[...] += contrib(2 * e + 1)

    res = alpha * h_ref[...] + acc_s[...]
    out_ref[...] = _layer_norm(res, g_ref[...], b_ref[...])


def _moe_layer(h, hb, w_router, wg, wu, wd, ln_g, ln_b, seq, *, alpha):
    B, Lp, D = h.shape
    E = N_EXPERTS
    F = wg.shape[-1]
    nj = seq // TL
    T = B * seq
    NB = T // TL
    NT = (2 * T) // TL + E
    NS = (NB * E + NT) // GATHER_BLOCKS + NT

    gate, rank = _router(h, w_router, seq)

    gate8 = gate[:, :E]
    rank8 = rank[:, :E]
    sel8 = gate8 > 0.0
    counts = rank8[-1] + sel8[-1].astype(jnp.int32)
    ntile = (counts + TL - 1) // TL
    tstart = jnp.cumsum(ntile) - ntile
    n_used = jnp.sum(ntile).astype(jnp.int32).reshape(1)
    slot8 = jnp.where(sel8, tstart[None, :] * TL + rank8, -1).astype(jnp.int32)
    slot_tm = jnp.pad(slot8, ((0, 0), (0, LANES - E)), constant_values=-1)
    slot_em = slot8.reshape(NB, TL, E).transpose(0, 2, 1)
    gate_em = gate8.reshape(NB, TL, E).transpose(0, 2, 1)
    tiles = jnp.arange(NT, dtype=jnp.int32)
    texp = jnp.sum((jnp.cumsum(ntile)[None, :] <= tiles[:, None]).astype(jnp.int32), axis=1)
    texp = jnp.minimum(texp, E - 1)

    cb = jnp.concatenate([rank8[::TL], counts[None, :]], axis=0)
    lo_be, hi_be = cb[:-1], cb[1:]
    blk_ids = jnp.arange(NB, dtype=jnp.int32)
    g_tile, g_b0, g_bl, g_nv, g_first, g_last = _gather_steps(
        lo_be, hi_be, counts, tstart, texp, n_used[0], NS)
    nonempty = hi_be > lo_be
    t0 = tstart[None, :] + lo_be // TL
    cross = nonempty & ((hi_be - 1) // TL > lo_be // TL)
    earlier = (blk_ids[None, :] <= blk_ids[:, None])[:, :, None]
    fill = lambda v: jnp.max(jnp.where(earlier, v[None, :, :], 0), axis=1)
    cidx = jnp.stack([fill(jnp.where(nonempty, t0, 0)), fill(jnp.where(cross, t0 + 1, 0))],
                     axis=-1).reshape(-1).astype(jnp.int32)
    cval = cross.reshape(-1).astype(jnp.int32)

    hblk = lambda blk: (blk // nj, blk % nj + 1, 0)
    src = lambda k: (lambda p, t, b0, bl, nv, f, l, x: jnp.minimum(b0[p] + k, bl[p]))
    of_step = lambda fn: (lambda p, t, b0, bl, nv, f, l, x: fn(t[p], x))
    h_spec = lambda k: pl.BlockSpec((None, TL, D), lambda *a: hblk(src(k)(*a)))
    em_spec = lambda k: pl.BlockSpec((None, E, TL), lambda *a: (src(k)(*a), 0, 0))
    w_spec = pl.BlockSpec((None, D, F), of_step(lambda t, x: (x[t], 0, 0)))
    nk = GATHER_BLOCKS

    act, gslot = pl.pallas_call(
        functools.partial(_expert_up_kernel, fc=_ffn_chunk(F)),
        grid_spec=pltpu.PrefetchScalarGridSpec(
            num_scalar_prefetch=7,
            grid=(NS,),
            in_specs=([h_spec(k) for k in range(nk)] + [em_spec(k) for k in range(nk)]
                      + [em_spec(k) for k in range(nk)] + [w_spec, w_spec]),
            out_specs=[pl.BlockSpec((TL, F), of_step(lambda t, x: (t, 0))),
                       pl.BlockSpec((TL, LANES), of_step(lambda t, x: (t, 0)))],
            scratch_shapes=[pltpu.VMEM((TL, D), BF16), pltpu.VMEM((TL, LANES), F32)],
        ),
        out_shape=[jax.ShapeDtypeStruct((NT * TL, F), BF16),
                   jax.ShapeDtypeStruct((NT * TL, LANES), F32)],
        compiler_params=pltpu.CompilerParams(
            dimension_semantics=("arbitrary",), vmem_limit_bytes=VMEM_LIMIT),
        name="moe_expert_up",
    )(g_tile, g_b0, g_bl, g_nv, g_first, g_last, texp, *([hb] * nk), *([slot_em] * nk),
      *([gate_em] * nk), wg.astype(BF16), wu.astype(BF16))

    used = lambda i, n: jnp.minimum(i, jnp.maximum(n[0] - 1, 0))
    ys = pl.pallas_call(
        _expert_down_kernel,
        grid_spec=pltpu.PrefetchScalarGridSpec(
            num_scalar_prefetch=2,
            grid=(NT,),
            in_specs=[
                pl.BlockSpec((TL, F), lambda i, x, n: (used(i, n), 0)),
                pl.BlockSpec((TL, LANES), lambda i, x, n: (used(i, n), 0)),
                pl.BlockSpec((None, F, D), lambda i, x, n: (x[used(i, n)], 0, 0)),
            ],
            out_specs=pl.BlockSpec((TL, D), lambda i, x, n: (used(i, n), 0)),
        ),
        out_shape=jax.ShapeDtypeStruct((NT * TL, D), BF16),
        compiler_params=pltpu.CompilerParams(
            dimension_semantics=("arbitrary",), vmem_limit_bytes=VMEM_LIMIT),
        name="moe_expert_down",
    )(texp, n_used, act, gslot, wd.astype(BF16))

    ypair = lambda k: pl.BlockSpec((TL, D), lambda b, ci, cv: (ci[b * 2 * E + k], 0))
    out = pl.pallas_call(
        functools.partial(_combine_kernel, alpha=alpha),
        grid_spec=pltpu.PrefetchScalarGridSpec(
            num_scalar_prefetch=2,
            grid=(NB,),
            in_specs=[ypair(k) for k in range(2 * E)] + [
                pl.BlockSpec((TL, LANES), lambda b, ci, cv: (b, 0)),
                pl.BlockSpec((None, TL, D), lambda b, ci, cv: hblk(b)),
                pl.BlockSpec((1, D), lambda b, ci, cv: (0, 0)),
                pl.BlockSpec((1, D), lambda b, ci, cv: (0, 0)),
            ],
            out_specs=pl.BlockSpec((None, TL, D), lambda b, ci, cv: (b // nj, b % nj, 0)),
            scratch_shapes=[pltpu.VMEM((TL, D), F32)],
        ),
        out_shape=jax.ShapeDtypeStruct((B, seq, D), F32),
        compiler_params=pltpu.CompilerParams(
            dimension_semantics=("arbitrary",), vmem_limit_bytes=VMEM_LIMIT),
        name="moe_combine",
    )(cidx, cval, *([ys] * (2 * E)), slot_tm, h,
      ln_g.reshape(1, D).astype(F32), ln_b.reshape(1, D).astype(F32))
    return out


def kernel(x, meta_tokens, ln_in_g, ln_in_b, w_in, lower_bounds, hg_norm_g, w_hg_out, conv_w,
           conv_b, conv_ln_g, conv_ln_b, w_conv_out, w_out, ln1_g, ln1_b, ffn_w_gate, ffn_w_up,
           ffn_w_down, moe_router, moe_w_gate, moe_w_up, moe_w_down, ln2_g, ln2_b):
    B, S, D = x.shape
    depth = w_in.shape[0]
    assert S % TL == 0 and D % LANES == 0 and meta_tokens.shape[0] == N_META
    alpha = float((2 * depth) ** 0.25)

    pr = jax.nn.softmax(lower_bounds.astype(F32), axis=0)
    lb_all = jnp.cumsum(pr, axis=0) - pr[0]

    h = _ln_in(x, meta_tokens, ln_in_g, ln_in_b)
    for l in range(depth):
        h, hb = _mix_layer(h, w_in[l], lb_all[l], hg_norm_g[l], w_hg_out[l], conv_w[l], conv_b[l],
                           conv_ln_g[l], conv_ln_b[l], w_conv_out[l], w_out[l], ln1_g[l],
                           ln1_b[l], alpha=alpha)
        jl = l // 2
        last = l == depth - 1
        if l % 2 == 0:
            h = _dense_ffn(h, hb, ffn_w_gate[jl], ffn_w_up[jl], ffn_w_down[jl], ln2_g[l],
                           ln2_b[l], alpha=alpha)
            if last:
                h = h[:, TL:]
        else:
            assert last, "MoE layers other than the last one would need the padded layout back"
            h = _moe_layer(h, hb, moe_router[jl], moe_w_gate[jl], moe_w_up[jl], moe_w_down[jl],
                           ln2_g[l], ln2_b[l], S, alpha=alpha)
    return h
```

```python
import functools

import jax
import jax.numpy as jnp
from jax import lax
from jax.experimental import pallas as pl
from jax.experimental.pallas import tpu as pltpu

F32 = jnp.float32
BF16 = jnp.bfloat16

N_META = 16
HG_HEADS = 8
HG_DK = 128
CONV_WIDTH = 31
N_EXPERTS = 8
LN_EPS = 1e-5
RMS_EPS = 1e-6

LANES = 128
SUBLANES = 8
TL = 256
HIST = 32
CONV_ROWS = 64
GATHER_BLOCKS = 8
PROJ_COLS = 256
VMEM_LIMIT = 56 * 1024 * 1024
LOG2E = 1.4426950408889634


def _sigmoid(x):
    return 0.5 * jnp.tanh(0.5 * x) + 0.5


def _layer_norm(x, g, b):
    mu = jnp.mean(x, axis=-1, keepdims=True)
    xc = x - mu
    var = jnp.mean(xc * xc, axis=-1, keepdims=True)
    return xc * lax.rsqrt(var + LN_EPS) * g + b


def _dot(a, b):
    return jnp.dot(a, b, preferred_element_type=F32)


def _dot_nt(a, b):
    return lax.dot_general(a, b, (((1,), (1,)), ((), ())), preferred_element_type=F32)


def _dot_tn(a, b):
    return lax.dot_general(a, b, (((0,), (0,)), ((), ())), preferred_element_type=F32)


def _const_spec(shape):
    nd = len(shape)
    return pl.BlockSpec(shape, lambda *_: (0,) * nd, pipeline_mode=pl.Buffered(1))


def _cumsum_rows(x):
    n = x.shape[0]
    hi = x.astype(BF16)
    r1 = x - hi.astype(F32)
    mid = r1.astype(BF16)
    lo = (r1 - mid.astype(F32)).astype(BF16)
    ri = lax.broadcasted_iota(jnp.int32, (n, n), 0)
    ci = lax.broadcasted_iota(jnp.int32, (n, n), 1)
    tri = (ri >= ci).astype(BF16)
    return _dot(tri, hi) + _dot(tri, mid) + _dot(tri, lo)


def _level_scores(qh, ch, bh, m):
    n = qh.shape[0]
    zero = jnp.zeros((m, qh.shape[1]), F32)
    qp, kp = [], []
    for jb in range(n // (2 * m)):
        lo, mid, hi = jb * 2 * m, jb * 2 * m + m, (jb + 1) * 2 * m
        e = bh[mid - 1:mid, :]
        kp += [jnp.exp2(e + ch[lo:mid]), zero]
        qp += [zero, qh[mid:hi] * jnp.exp2(bh[mid:hi] - e)]
    qp = jnp.concatenate(qp, axis=0).astype(BF16)
    kp = jnp.concatenate(kp, axis=0).astype(BF16)
    return _dot_nt(qp, kp)


def _mix_kernel(*refs, pad, alpha, fuse_ln_in):
    n_in = 4 if fuse_ln_in else 1
    tile_in, refs = refs[:n_in], refs[n_in:]
    out_ref, outb_ref = refs[12], refs[13]
    st_s, hist_s = refs[14], refs[15]
    hv_s, stm_s, histm_s = refs[-3:]
    b = pl.program_id(0)
    j = pl.program_id(1)
    first = j == 0

    @pl.when(first & (b == 0))
    def _():
        st_s[...] = jnp.zeros_like(st_s)
        hist_s[0:HIST, :] = jnp.zeros((HIST, hist_s.shape[1]), F32)

    @pl.when(first & (b > 0))
    def _():
        st_s[...] = stm_s[...]
        hist_s[0:HIST, :] = histm_s[...]
        out_ref[...] = jnp.zeros_like(out_ref)
        outb_ref[...] = jnp.zeros_like(outb_ref)

    @pl.when(jnp.logical_not(first) | (b == 0))
    def _():
        rows = lax.broadcasted_iota(jnp.int32, (TL, 1), 0)
        if fuse_ln_in:
            x_ref, meta_ref, g_ref, b_ref = tile_in

            @pl.when(first)
            def _():
                y = _layer_norm(meta_ref[...], g_ref[...], b_ref[...])
                hv_s[...] = jnp.where(rows >= pad, y, 0.0)

            @pl.when(jnp.logical_not(first))
            def _():
                hv_s[...] = _layer_norm(x_ref[...], g_ref[...], b_ref[...])
        else:
            hv_s[...] = jnp.where(j * TL + rows >= pad, tile_in[0][...], 0.0)

        _mix_tile(*refs[:-2], alpha=alpha)

        @pl.when(first)
        def _():
            stm_s[...] = st_s[...]
            histm_s[...] = hist_s[0:HIST, :]


def _mix_tile(w_in_ref, lb_ref, hgg_ref, w_hg_ref, cw_ref, cb_ref, clg_ref, clb_ref,
              w_cv_ref, w_out_ref, ln_g_ref, ln_b_ref, out_ref, outb_ref,
              st_s, hist_s, hb_s, q_s, c_s, b_s, v_s, og_s, ob_s, sh_s, hv_s, *, alpha):
    D = hv_s.shape[-1]
    H = HG_HEADS
    HALF = TL // 2
    hb_s[...] = hv_s[...].astype(BF16)

    def proj(g, c0=0, width=None):
        lo = g * D + c0
        return _dot(hb_s[...], w_in_ref[:, lo:lo + (D if width is None else width)])


    for c0 in range(0, D, PROJ_COLS):
        cols = slice(c0, c0 + PROJ_COLS)

        def split_heads(dst, val):
            for i in range(PROJ_COLS // LANES):
                dst[c0 // LANES + i] = val[:, i * LANES:(i + 1) * LANES]

        p = proj(0, c0, PROJ_COLS)
        split_heads(q_s, p * (jnp.tanh(p) + 1.0))

        z = proj(1, c0, PROJ_COLS)
        lb = lb_ref[:, cols]
        one_m_lb = 1.0 - lb
        e = jnp.exp2(jnp.abs(z) * (-LOG2E))
        log_sz = jnp.minimum(z, 0.0) - jnp.log(1.0 + e)
        f = lb + one_m_lb * jnp.exp(log_sz)
        log_f = jnp.where(lb > 0.0, jnp.log(f), log_sz)
        log_k = (log_sz - z) + jnp.log(one_m_lb)
        bq = _cumsum_rows(log_f) * LOG2E
        split_heads(b_s, bq)
        split_heads(c_s, log_k * LOG2E - bq)

        split_heads(v_s, proj(2, c0, PROJ_COLS))
        p = proj(3, c0, PROJ_COLS)
        split_heads(og_s, p * (jnp.tanh(p) + 1.0))
        hist_s[HIST:HIST + TL, cols] = (proj(4, c0, PROJ_COLS)
                                        * (jnp.tanh(proj(5, c0, PROJ_COLS)) + 1.0))

    ri = lax.broadcasted_iota(jnp.int32, (HALF, HALF), 0)
    ci = lax.broadcasted_iota(jnp.int32, (HALF, HALF), 1)
    lane8 = lax.broadcasted_iota(jnp.int32, (SUBLANES, LANES), 1)
    sub8 = lax.broadcasted_iota(jnp.int32, (SUBLANES, LANES), 0)

    def head_body(hd, carry):
        q = q_s[hd]
        bc = b_s[hd]
        cc = c_s[hd]
        vb = v_s[hd].astype(BF16)
        st = st_s[hd]
        b_last = bc[TL - 1:TL, :]

        o_inter = _dot_nt((q * jnp.exp2(bc)).astype(BF16), st.astype(BF16))
        k_end = jnp.exp2(b_last + cc).astype(BF16)
        st_s[hd] = st * jnp.exp2(b_last) + _dot_tn(vb, k_end)

        def diag_block(r0):
            q8, b8 = q[r0:r0 + SUBLANES], bc[r0:r0 + SUBLANES]
            c0 = r0 % HALF
            acc = jnp.zeros((SUBLANES, LANES), F32)
            for s in range(SUBLANES):
                w = jnp.exp2(b8 + c_s[hd, r0 + s:r0 + s + 1, :])
                col = jnp.sum(q8 * w, axis=-1, keepdims=True)
                acc = jnp.where(lane8 == c0 + s, col, acc)
            keep = (lane8 >= c0) & (lane8 - c0 <= sub8)
            return jnp.where(keep, acc, 0.0)

        b_mid = bc[HALF - 1:HALF, :]
        a10 = _dot_nt((q[HALF:] * jnp.exp2(bc[HALF:] - b_mid)).astype(BF16),
                      jnp.exp2(b_mid + cc[:HALF]).astype(BF16))
        atts = []
        for hh in range(2):
            sl = slice(hh * HALF, (hh + 1) * HALF)
            qh, ch, bh = q[sl], cc[sl], bc[sl]
            att = jnp.concatenate(
                [diag_block(hh * HALF + i * SUBLANES) for i in range(HALF // SUBLANES)], axis=0)
            m = HALF // 2
            while m >= SUBLANES:
                sib = (ri & -(2 * m)) == (ci & -(2 * m))
                att = att + jnp.where(sib, _level_scores(qh, ch, bh, m), 0.0)
                m //= 2
            atts.append(att)
        o0 = _dot(atts[0].astype(BF16), vb[:HALF])
        o1 = _dot(jnp.concatenate([a10, atts[1]], axis=1).astype(BF16), vb)
        o = o_inter + jnp.concatenate([o0, o1], axis=0)

        ms = jnp.mean(o * o, axis=-1, keepdims=True)
        ob_s[hd] = (o * lax.rsqrt(ms + RMS_EPS) * hgg_ref[hd] * og_s[hd]).astype(BF16)
        return carry

    for hd in range(H):
        head_body(hd, 0)

    y_rec = _dot(jnp.concatenate([ob_s[hd] for hd in range(H)], axis=1), w_hg_ref[...])

    base = HIST - (CONV_WIDTH - 1)
    nshift = HIST + TL - SUBLANES
    strips = []
    for c in range(D // LANES):
        cs = slice(c * LANES, (c + 1) * LANES)
        for r in range(1, SUBLANES):
            sh_s[r - 1] = hist_s[r:r + nshift, cs]
        chunks = []
        for rc in range(0, TL, CONV_ROWS):
            acc = jnp.broadcast_to(cb_ref[:, cs], (CONV_ROWS, LANES))
            for t in range(CONV_WIDTH):
                a, r = divmod(base + t, SUBLANES)
                lo = a * SUBLANES + rc
                src = hist_s[lo:lo + CONV_ROWS, cs] if r == 0 else sh_s[r - 1, lo:lo + CONV_ROWS, :]
                acc = acc + src * cw_ref[t:t + 1, cs]
            chunks.append(acc)
        strips.append(jnp.concatenate(chunks, axis=0))
    yc = jnp.concatenate(strips, axis=1)
    hist_s[0:HIST, :] = hist_s[TL:TL + HIST, :]
    yc = _layer_norm(yc, clg_ref[...], clb_ref[...])
    y_conv = _dot((yc * (jnp.tanh(yc) + 1.0)).astype(BF16), w_cv_ref[...])

    mixed = (jnp.tanh(proj(6)) + 1.0) * y_rec + (jnp.tanh(proj(7)) + 1.0) * y_conv
    res = alpha * hv_s[...] + _dot(mixed.astype(BF16), w_out_ref[...])
    hn = _layer_norm(res, ln_g_ref[...], ln_b_ref[...])
    out_ref[...] = hn
    outb_ref[...] = hn.astype(BF16)


def _mix_layer(tile_in, w_in, lb, hg_g, w_hg, conv_w, conv_b, cln_g, cln_b, w_cv, w_out, ln_g,
               ln_b, *, alpha):
    row = lambda a: a.reshape(1, -1).astype(F32)
    fuse_ln_in = len(tile_in) == 4
    if fuse_ln_in:
        x, meta_tokens, g_in, b_in = tile_in
        B, S, D = x.shape
        Lp = S + TL
        meta_pad = jnp.pad(meta_tokens.astype(F32), ((TL - N_META, 0), (0, 0)))
        tile_args = (x, meta_pad, row(g_in), row(b_in))
        tile_specs = [
            pl.BlockSpec((None, TL, D), lambda bi, j: (bi, jnp.maximum(j - 1, 0), 0)),
            _const_spec((TL, D)), _const_spec((1, D)), _const_spec((1, D))]
    else:
        B, Lp, D = tile_in[0].shape
        tile_args = tile_in
        tile_specs = [pl.BlockSpec((None, TL, D), lambda bi, j: (bi, j, 0))]
    H = HG_HEADS
    nt = Lp // TL
    col_half = jnp.concatenate([
        jnp.full((D,), 0.5, F32), jnp.ones((2 * D,), F32), jnp.full((5 * D,), 0.5, F32)])
    w_in = w_in * col_half
    cln_g, cln_b, w_out = 0.5 * cln_g, 0.5 * cln_b, 0.5 * w_out
    return pl.pallas_call(
        functools.partial(_mix_kernel, pad=TL - N_META, alpha=alpha, fuse_ln_in=fuse_ln_in),
        grid=(B, nt),
        in_specs=tile_specs + [
            _const_spec(w_in.shape),
            _const_spec((1, D)),
            _const_spec((H, 1, LANES)),
            _const_spec((D, D)),
            _const_spec((CONV_WIDTH, D)),
            _const_spec((1, D)),
            _const_spec((1, D)),
            _const_spec((1, D)),
            _const_spec((D, D)),
            _const_spec((D, D)),
            _const_spec((1, D)),
            _const_spec((1, D)),
        ],
        out_specs=[pl.BlockSpec((None, TL, D), lambda bi, j: (bi, j, 0)),
                   pl.BlockSpec((None, TL, D), lambda bi, j: (bi, j, 0))],
        out_shape=[jax.ShapeDtypeStruct((B, Lp, D), F32),
                   jax.ShapeDtypeStruct((B, Lp, D), BF16)],
        scratch_shapes=[
            pltpu.VMEM((H, HG_DK, HG_DK), F32),
            pltpu.VMEM((HIST + TL, D), F32),
            pltpu.VMEM((TL, D), BF16),
            pltpu.VMEM((H, TL, LANES), F32),
            pltpu.VMEM((H, TL, LANES), F32),
            pltpu.VMEM((H, TL, LANES), F32),
            pltpu.VMEM((H, TL, LANES), F32),
            pltpu.VMEM((H, TL, LANES), F32),
            pltpu.VMEM((H, TL, LANES), BF16),
            pltpu.VMEM((SUBLANES - 1, HIST + TL - SUBLANES, LANES), F32),
            pltpu.VMEM((TL, D), F32),
            pltpu.VMEM((H, HG_DK, HG_DK), F32),
            pltpu.VMEM((HIST, D), F32),
        ],
        compiler_params=pltpu.CompilerParams(
            dimension_semantics=("arbitrary", "arbitrary"),
            vmem_limit_bytes=VMEM_LIMIT),
        name="mix_layer",
    )(*tile_args, w_in.astype(BF16), row(lb), hg_g.reshape(H, 1, LANES).astype(F32), w_hg.astype(BF16),
      conv_w.astype(F32), row(conv_b), row(cln_g), row(cln_b), w_cv.astype(BF16),
      w_out.astype(BF16), row(ln_g), row(ln_b))


def _ffn_kernel(h_ref, hb_ref, wg_ref, wu_ref, wd_ref, g_ref, b_ref, out_ref, a_s, *, alpha, fc,
                tiles_per_seq):
    i = pl.program_id(0)
    dead = (i % tiles_per_seq == 0) & (i > 0)

    @pl.when(dead)
    def _():
        out_ref[...] = jnp.zeros_like(out_ref)

    @pl.when(jnp.logical_not(dead))
    def _():
        xb = hb_ref[...]
        F = wg_ref.shape[1]
        for c in range(F // fc):
            cs = slice(c * fc, (c + 1) * fc)
            g = _dot(xb, wg_ref[:, cs])
            u = _dot(xb, wu_ref[:, cs])
            a_s[:, cs] = (g * _sigmoid(g) * u).astype(BF16)
        res = alpha * h_ref[...] + _dot(a_s[...], wd_ref[...])
        out_ref[...] = _layer_norm(res, g_ref[...], b_ref[...])


def _ffn_chunk(F):
    for fc in (512, 1408, 1024, 256, 128):
        if F % fc == 0:
            return fc
    return F


def _dense_ffn(h, hb, wg, wu, wd, ln_g, ln_b, *, alpha):
    B, Lp, D = h.shape
    F = wg.shape[1]
    R = B * Lp
    out = pl.pallas_call(
        functools.partial(_ffn_kernel, alpha=alpha, fc=_ffn_chunk(F), tiles_per_seq=Lp // TL),
        grid=(R // TL,),
        in_specs=[
            pl.BlockSpec((TL, D), lambda i: (i, 0)),
            pl.BlockSpec((TL, D), lambda i: (i, 0)),
            _const_spec((D, F)),
            _const_spec((D, F)),
            _const_spec((F, D)),
            _const_spec((1, D)),
            _const_spec((1, D)),
        ],
        out_specs=pl.BlockSpec((TL, D), lambda i: (i, 0)),
        out_shape=jax.ShapeDtypeStruct((R, D), F32),
        scratch_shapes=[pltpu.VMEM((TL, F), BF16)],
        compiler_params=pltpu.CompilerParams(
            dimension_semantics=("arbitrary",), vmem_limit_bytes=VMEM_LIMIT),
        name="dense_ffn",
    )(h.reshape(R, D), hb.reshape(R, D), wg.astype(BF16), wu.astype(BF16), wd.astype(BF16),
      ln_g.reshape(1, D).astype(F32), ln_b.reshape(1, D).astype(F32))
    return out.reshape(B, Lp, D)


def _router_kernel(h_ref, wr_ref, gate_ref, rank_ref, cnt_s):
    first = (pl.program_id(0) == 0) & (pl.program_id(1) == 0)

    @pl.when(first)
    def _():
        cnt_s[...] = jnp.zeros_like(cnt_s)

    hv, wv = h_ref[...], wr_ref[...]
    h_hi, w_hi = hv.astype(BF16), wv.astype(BF16)
    h_lo = (hv - h_hi.astype(F32)).astype(BF16)
    w_lo = (wv - w_hi.astype(F32)).astype(BF16)
    logits = _dot(h_hi, w_hi) + (_dot(h_hi, w_lo) + _dot(h_lo, w_hi))
    lane = lax.broadcasted_iota(jnp.int32, (TL, LANES), 1).astype(F32)
    neg = jnp.float32(-jnp.inf)
    logits = jnp.where(lane < N_EXPERTS, logits, neg)
    m1 = jnp.max(logits, axis=-1, keepdims=True)
    i1 = jnp.min(jnp.where(logits == m1, lane, float(LANES)), axis=-1, keepdims=True)
    oh1 = lane == i1
    rest = jnp.where(oh1, neg, logits)
    m2 = jnp.max(rest, axis=-1, keepdims=True)
    i2 = jnp.min(jnp.where(rest == m2, lane, float(LANES)), axis=-1, keepdims=True)
    oh2 = lane == i2
    e = jnp.exp(m2 - m1)
    w1 = 1.0 / (1.0 + e)
    w2 = e * w1
    gates = jnp.where(oh1, w1, 0.0) + jnp.where(oh2, w2, 0.0)
    gate_ref[...] = gates
    sel = (gates > 0.0).astype(BF16)
    ri = lax.broadcasted_iota(jnp.int32, (TL, TL), 0)
    ci = lax.broadcasted_iota(jnp.int32, (TL, TL), 1)
    before = _dot((ci < ri).astype(BF16), sel)
    cnt = cnt_s[...]
    rank_ref[...] = (before + cnt).astype(jnp.int32)
    cnt_s[...] = cnt + before[TL - 1:TL, :] + sel[TL - 1:TL, :].astype(F32)


def _router(h, w_router, seq):
    B, Lp, D = h.shape
    nj = seq // TL
    T = B * seq
    wr = jnp.pad(w_router.astype(F32), ((0, 0), (0, LANES - N_EXPERTS)))
    return pl.pallas_call(
        _router_kernel,
        grid=(B, nj),
        in_specs=[
            pl.BlockSpec((None, TL, D), lambda bi, j: (bi, j + 1, 0)),
            _const_spec((D, LANES)),
        ],
        out_specs=[
            pl.BlockSpec((TL, LANES), lambda bi, j: (bi * nj + j, 0)),
            pl.BlockSpec((TL, LANES), lambda bi, j: (bi * nj + j, 0)),
        ],
        out_shape=[jax.ShapeDtypeStruct((T, LANES), F32),
                   jax.ShapeDtypeStruct((T, LANES), jnp.int32)],
        scratch_shapes=[pltpu.VMEM((1, LANES), F32)],
        compiler_params=pltpu.CompilerParams(
            dimension_semantics=("arbitrary", "arbitrary")),
        name="moe_router",
    )(h, wr)


def _gather_steps(lo_be, hi_be, counts, tstart, texp, n_used, n_steps):
    nb, ne = lo_be.shape
    nt = texp.shape[0]
    tiles = jnp.arange(nt, dtype=jnp.int32)
    of_tile = texp[:, None] == jnp.arange(ne, dtype=jnp.int32)[None, :]
    pick_e = lambda a: jnp.sum(jnp.where(of_tile[:, :, None], a.T[None, :, :], 0), axis=1)
    pick_v = lambda v: jnp.sum(jnp.where(of_tile, v[None, :], 0), axis=1)
    s0 = (tiles - pick_v(tstart)) * TL
    s1 = jnp.minimum(s0 + TL, pick_v(counts))
    ov = (pick_e(lo_be) < s1[:, None]) & (pick_e(hi_be) > s0[:, None]) & (tiles < n_used)[:, None]
    nblk = jnp.sum(ov.astype(jnp.int32), axis=1)
    blk_ids = jnp.arange(nb, dtype=jnp.int32)
    b_first = jnp.min(jnp.where(ov, blk_ids[None, :], nb), axis=1)
    nstep = (nblk + GATHER_BLOCKS - 1) // GATHER_BLOCKS
    end = jnp.cumsum(nstep)
    total = end[-1]
    start = end - nstep
    s = jnp.arange(n_steps, dtype=jnp.int32)
    sc = jnp.minimum(s, total - 1)
    tile = jnp.sum((end[None, :] <= sc[:, None]).astype(jnp.int32), axis=1)
    hit = tile[:, None] == tiles[None, :]
    pick_t = lambda a: jnp.sum(jnp.where(hit, a[None, :], 0), axis=1)
    u = sc - pick_t(start)
    valid = s < total
    b0 = pick_t(b_first) + GATHER_BLOCKS * u
    b_last = pick_t(b_first + nblk - 1)
    nv = jnp.where(valid, jnp.minimum(GATHER_BLOCKS, pick_t(nblk) - GATHER_BLOCKS * u), 0)
    first = valid & (u == 0)
    last = valid & (u == pick_t(nstep) - 1)
    i32 = lambda a: a.astype(jnp.int32)
    return i32(tile), i32(b0), i32(b_last), i32(nv), i32(first), i32(last)


def _expert_up_kernel(tile_ref, b0_ref, bl_ref, nv_ref, first_ref, last_ref, texp_ref, *refs, fc):
    nk = GATHER_BLOCKS
    h_refs, route_refs = refs[:nk], refs[nk:2 * nk]
    wg_ref, wu_ref, a_ref, gs_ref, x_s, g_s = refs[2 * nk:]
    p = pl.program_id(0)
    tile = tile_ref[p]
    e = texp_ref[tile]

    @pl.when(first_ref[p] == 1)
    def _():
        x_s[...] = jnp.zeros_like(x_s)
        g_s[...] = jnp.zeros_like(g_s)

    for k in range(nk):
        @pl.when(k < nv_ref[p])
        def _(k=k):
            srow = route_refs[k][pl.ds(e, 1), :] - tile * TL
            grow = pltpu.bitcast(route_refs[k][pl.ds(N_EXPERTS + e, 1), :], F32)
            ri = lax.broadcasted_iota(jnp.int32, (TL, TL), 0)
            hit = ri == srow
            x_s[...] += _dot(hit.astype(BF16), h_refs[k][...]).astype(BF16)
            g_s[...] += jnp.sum(jnp.where(hit, grow, 0.0), axis=-1, keepdims=True)

    @pl.when(last_ref[p] == 1)
    def _():
        xb = x_s[...]
        F = wg_ref.shape[1]
        for c in range(F // fc):
            cs = slice(c * fc, (c + 1) * fc)
            g = _dot(xb, wg_ref[:, cs])
            u = _dot(xb, wu_ref[:, cs])
            a_ref[:, cs] = (g * _sigmoid(g) * u).astype(BF16)
        gs_ref[...] = g_s[...]


def _expert_down_kernel(texp_ref, nused_ref, a_ref, gs_ref, wd_ref, y_ref):
    i = pl.program_id(0)

    @pl.when(i < nused_ref[0])
    def _():
        y_ref[...] = (_dot(a_ref[...], wd_ref[...]) * gs_ref[:, 0:1]).astype(BF16)


def _combine_kernel(cidx_ref, cval_ref, *refs, alpha):
    npair = 2 * N_EXPERTS
    y_refs = refs[:npair]
    slot_ref, h_ref, g_ref, b_ref, out_ref, acc_s = refs[npair:]
    b = pl.program_id(0)
    ci = lax.broadcasted_iota(jnp.int32, (TL, TL), 1)

    def contrib(k):
        e = k // 2
        scol = slot_ref[:, e:e + 1] - cidx_ref[b * npair + k] * TL
        return _dot((ci == scol).astype(BF16), y_refs[k][...])

    acc = contrib(0)
    for e in range(1, N_EXPERTS):
        acc = acc + contrib(2 * e)
    acc_s[...] = acc
    for e in range(N_EXPERTS):
        @pl.when(cval_ref[b * N_EXPERTS + e] == 1)
        def _(e=e):
            acc_s[...] += contrib(2 * e + 1)

    res = alpha * h_ref[...] + acc_s[...]
    out_ref[...] = _layer_norm(res, g_ref[...], b_ref[...])


def _moe_layer(h, hb, w_router, wg, wu, wd, ln_g, ln_b, seq, *, alpha):
    B, Lp, D = h.shape
    E = N_EXPERTS
    F = wg.shape[-1]
    nj = seq // TL
    T = B * seq
    NB = T // TL
    NT = (2 * T) // TL + E
    NS = -(-(NB * E) // GATHER_BLOCKS) + NT

    gate, rank = _router(h, w_router, seq)

    gate8 = gate[:, :E]
    rank8 = rank[:, :E]
    sel8 = gate8 > 0.0
    counts = rank8[-1] + sel8[-1].astype(jnp.int32)
    ntile = (counts + TL - 1) // TL
    tstart = jnp.cumsum(ntile) - ntile
    n_used = jnp.sum(ntile).astype(jnp.int32).reshape(1)
    slot8 = jnp.where(sel8, tstart[None, :] * TL + rank8, -1).astype(jnp.int32)
    slot_tm = jnp.pad(slot8, ((0, 0), (0, LANES - E)), constant_values=-1)
    route = jnp.concatenate([slot8, lax.bitcast_convert_type(gate8, jnp.int32)], axis=1)
    route_em = route.reshape(NB, TL, 2 * E).transpose(0, 2, 1)
    tiles = jnp.arange(NT, dtype=jnp.int32)
    texp = jnp.sum((jnp.cumsum(ntile)[None, :] <= tiles[:, None]).astype(jnp.int32), axis=1)
    texp = jnp.minimum(texp, E - 1)

    cb = jnp.concatenate([rank8[::TL], counts[None, :]], axis=0)
    lo_be, hi_be = cb[:-1], cb[1:]
    blk_ids = jnp.arange(NB, dtype=jnp.int32)
    g_tile, g_b0, g_bl, g_nv, g_first, g_last = _gather_steps(
        lo_be, hi_be, counts, tstart, texp, n_used[0], NS)
    nonempty = hi_be > lo_be
    t0 = tstart[None, :] + lo_be // TL
    cross = nonempty & ((hi_be - 1) // TL > lo_be // TL)
    earlier = (blk_ids[None, :] <= blk_ids[:, None])[:, :, None]
    fill = lambda v: jnp.max(jnp.where(earlier, v[None, :, :], 0), axis=1)
    cidx = jnp.stack([fill(jnp.where(nonempty, t0, 0)), fill(jnp.where(cross, t0 + 1, 0))],
                     axis=-1).reshape(-1).astype(jnp.int32)
    cval = cross.reshape(-1).astype(jnp.int32)

    hblk = lambda blk: (blk // nj, blk % nj + 1, 0)
    src = lambda k: (lambda p, t, b0, bl, nv, f, l, x: jnp.minimum(b0[p] + k, bl[p]))
    of_step = lambda fn: (lambda p, t, b0, bl, nv, f, l, x: fn(t[p], x))
    h_spec = lambda k: pl.BlockSpec((None, TL, D), lambda *a: hblk(src(k)(*a)))
    em_spec = lambda k: pl.BlockSpec((None, 2 * E, TL), lambda *a: (src(k)(*a), 0, 0))
    w_spec = pl.BlockSpec((None, D, F), of_step(lambda t, x: (x[t], 0, 0)))
    nk = GATHER_BLOCKS

    act, gslot = pl.pallas_call(
        functools.partial(_expert_up_kernel, fc=_ffn_chunk(F)),
        grid_spec=pltpu.PrefetchScalarGridSpec(
            num_scalar_prefetch=7,
            grid=(NS,),
            in_specs=([h_spec(k) for k in range(nk)] + [em_spec(k) for k in range(nk)]
                      + [w_spec, w_spec]),
            out_specs=[pl.BlockSpec((TL, F), of_step(lambda t, x: (t, 0))),
                       pl.BlockSpec((TL, LANES), of_step(lambda t, x: (t, 0)))],
            scratch_shapes=[pltpu.VMEM((TL, D), BF16), pltpu.VMEM((TL, LANES), F32)],
        ),
        out_shape=[jax.ShapeDtypeStruct((NT * TL, F), BF16),
                   jax.ShapeDtypeStruct((NT * TL, LANES), F32)],
        compiler_params=pltpu.CompilerParams(
            dimension_semantics=("arbitrary",), vmem_limit_bytes=VMEM_LIMIT),
        name="moe_expert_up",
    )(g_tile, g_b0, g_bl, g_nv, g_first, g_last, texp, *([hb] * nk), *([route_em] * nk),
      wg.astype(BF16), wu.astype(BF16))

    used = lambda i, n: jnp.minimum(i, jnp.maximum(n[0] - 1, 0))
    ys = pl.pallas_call(
        _expert_down_kernel,
        grid_spec=pltpu.PrefetchScalarGridSpec(
            num_scalar_prefetch=2,
            grid=(NT,),
            in_specs=[
                pl.BlockSpec((TL, F), lambda i, x, n: (used(i, n), 0)),
                pl.BlockSpec((TL, LANES), lambda i, x, n: (used(i, n), 0)),
                pl.BlockSpec((None, F, D), lambda i, x, n: (x[used(i, n)], 0, 0)),
            ],
            out_specs=pl.BlockSpec((TL, D), lambda i, x, n: (used(i, n), 0)),
        ),
        out_shape=jax.ShapeDtypeStruct((NT * TL, D), BF16),
        compiler_params=pltpu.CompilerParams(
            dimension_semantics=("arbitrary",), vmem_limit_bytes=VMEM_LIMIT),
        name="moe_expert_down",
    )(texp, n_used, act, gslot, wd.astype(BF16))

    ypair = lambda k: pl.BlockSpec((TL, D), lambda b, ci, cv: (ci[b * 2 * E + k], 0))
    out = pl.pallas_call(
        functools.partial(_combine_kernel, alpha=alpha),
        grid_spec=pltpu.PrefetchScalarGridSpec(
            num_scalar_prefetch=2,
            grid=(NB,),
            in_specs=[ypair(k) for k in range(2 * E)] + [
                pl.BlockSpec((TL, LANES), lambda b, ci, cv: (b, 0)),
                pl.BlockSpec((None, TL, D), lambda b, ci, cv: hblk(b)),
                pl.BlockSpec((1, D), lambda b, ci, cv: (0, 0)),
                pl.BlockSpec((1, D), lambda b, ci, cv: (0, 0)),
            ],
            out_specs=pl.BlockSpec((None, TL, D), lambda b, ci, cv: (b // nj, b % nj, 0)),
            scratch_shapes=[pltpu.VMEM((TL, D), F32)],
        ),
        out_shape=jax.ShapeDtypeStruct((B, seq, D), F32),
        compiler_params=pltpu.CompilerParams(
            dimension_semantics=("arbitrary",), vmem_limit_bytes=VMEM_LIMIT),
        name="moe_combine",
    )(cidx, cval, *([ys] * (2 * E)), slot_tm, h,
      ln_g.reshape(1, D).astype(F32), ln_b.reshape(1, D).astype(F32))
    return out


def kernel(x, meta_tokens, ln_in_g, ln_in_b, w_in, lower_bounds, hg_norm_g, w_hg_out, conv_w,
           conv_b, conv_ln_g, conv_ln_b, w_conv_out, w_out, ln1_g, ln1_b, ffn_w_gate, ffn_w_up,
           ffn_w_down, moe_router, moe_w_gate, moe_w_up, moe_w_down, ln2_g, ln2_b):
    B, S, D = x.shape
    depth = w_in.shape[0]
    assert S % TL == 0 and D % LANES == 0 and meta_tokens.shape[0] == N_META
    alpha = float((2 * depth) ** 0.25)

    pr = jax.nn.softmax(lower_bounds.astype(F32), axis=0)
    lb_all = jnp.cumsum(pr, axis=0) - pr[0]

    tile_in = (x, meta_tokens, ln_in_g, ln_in_b)
    for l in range(depth):
        h, hb = _mix_layer(tile_in if l == 0 else (h,), w_in[l], lb_all[l], hg_norm_g[l], w_hg_out[l], conv_w[l], conv_b[l],
                           conv_ln_g[l], conv_ln_b[l], w_conv_out[l], w_out[l], ln1_g[l],
                           ln1_b[l], alpha=alpha)
        jl = l // 2
        last = l == depth - 1
        if l % 2 == 0:
            h = _dense_ffn(h, hb, ffn_w_gate[jl], ffn_w_up[jl], ffn_w_down[jl], ln2_g[l],
                           ln2_b[l], alpha=alpha)
            if last:
                h = h[:, TL:]
        else:
            assert last, "MoE layers other than the last one would need the padded layout back"
            h = _moe_layer(h, hb, moe_router[jl], moe_w_gate[jl], moe_w_up[jl], moe_w_down[jl],
                           ln2_g[l], ln2_b[l], S, alpha=alpha)
    return h
```

```python
import functools

import jax
import jax.numpy as jnp
from jax import lax
from jax.experimental import pallas as pl
from jax.experimental.pallas import tpu as pltpu

F32 = jnp.float32
BF16 = jnp.bfloat16

N_META = 16
HG_HEADS = 8
HG_DK = 128
CONV_WIDTH = 31
N_EXPERTS = 8
LN_EPS = 1e-5
RMS_EPS = 1e-6

LANES = 128
SUBLANES = 8
TL = 256
HIST = 32
CONV_ROWS = 64
GATHER_BLOCKS = 8
PROJ_COLS = 256
VMEM_LIMIT = 56 * 1024 * 1024
LOG2E = 1.4426950408889634


def _sigmoid(x):
    return 0.5 * jnp.tanh(0.5 * x) + 0.5


def _layer_norm(x, g, b):
    mu = jnp.mean(x, axis=-1, keepdims=True)
    xc = x - mu
    var = jnp.mean(xc * xc, axis=-1, keepdims=True)
    return xc * lax.rsqrt(var + LN_EPS) * g + b


def _dot(a, b):
    return jnp.dot(a, b, preferred_element_type=F32)


def _dot_nt(a, b):
    return lax.dot_general(a, b, (((1,), (1,)), ((), ())), preferred_element_type=F32)


def _dot_tn(a, b):
    return lax.dot_general(a, b, (((0,), (0,)), ((), ())), preferred_element_type=F32)


def _const_spec(shape):
    nd = len(shape)
    return pl.BlockSpec(shape, lambda *_: (0,) * nd, pipeline_mode=pl.Buffered(1))


def _cumsum_rows(x):
    n = x.shape[0]
    hi = x.astype(BF16)
    r1 = x - hi.astype(F32)
    mid = r1.astype(BF16)
    lo = (r1 - mid.astype(F32)).astype(BF16)
    ri = lax.broadcasted_iota(jnp.int32, (n, n), 0)
    ci = lax.broadcasted_iota(jnp.int32, (n, n), 1)
    tri = (ri >= ci).astype(BF16)
    return _dot(tri, hi) + _dot(tri, mid) + _dot(tri, lo)


def _level_scores(qh, ch, bh, m):
    n = qh.shape[0]
    zero = jnp.zeros((m, qh.shape[1]), F32)
    qp, kp = [], []
    for jb in range(n // (2 * m)):
        lo, mid, hi = jb * 2 * m, jb * 2 * m + m, (jb + 1) * 2 * m
        e = bh[mid - 1:mid, :]
        kp += [jnp.exp2(e + ch[lo:mid]), zero]
        qp += [zero, qh[mid:hi] * jnp.exp2(bh[mid:hi] - e)]
    qp = jnp.concatenate(qp, axis=0).astype(BF16)
    kp = jnp.concatenate(kp, axis=0).astype(BF16)
    return _dot_nt(qp, kp)


def _mix_kernel(*refs, pad, alpha, fuse_ln_in):
    n_in = 4 if fuse_ln_in else 1
    tile_in, refs = refs[:n_in], refs[n_in:]
    out_ref, outb_ref = refs[12], refs[13]
    st_s, hist_s = refs[14], refs[15]
    hv_s, stm_s, histm_s = refs[-3:]
    b = pl.program_id(0)
    j = pl.program_id(1)
    first = j == 0

    @pl.when(first & (b == 0))
    def _():
        st_s[...] = jnp.zeros_like(st_s)
        hist_s[0:HIST, :] = jnp.zeros((HIST, hist_s.shape[1]), F32)

    @pl.when(first & (b > 0))
    def _():
        st_s[...] = stm_s[...]
        hist_s[0:HIST, :] = histm_s[...]
        out_ref[...] = jnp.zeros_like(out_ref)
        outb_ref[...] = jnp.zeros_like(outb_ref)

    @pl.when(jnp.logical_not(first) | (b == 0))
    def _():
        rows = lax.broadcasted_iota(jnp.int32, (TL, 1), 0)
        if fuse_ln_in:
            x_ref, meta_ref, g_ref, b_ref = tile_in

            @pl.when(first)
            def _():
                y = _layer_norm(meta_ref[...], g_ref[...], b_ref[...])
                hv_s[...] = jnp.where(rows >= pad, y, 0.0)

            @pl.when(jnp.logical_not(first))
            def _():
                hv_s[...] = _layer_norm(x_ref[...], g_ref[...], b_ref[...])
        else:
            hv_s[...] = jnp.where(j * TL + rows >= pad, tile_in[0][...], 0.0)

        _mix_tile(*refs[:-2], alpha=alpha)

        @pl.when(first)
        def _():
            stm_s[...] = st_s[...]
            histm_s[...] = hist_s[0:HIST, :]


def _mix_tile(w_in_ref, lb_ref, hgg_ref, w_hg_ref, cw_ref, cb_ref, clg_ref, clb_ref,
              w_cv_ref, w_out_ref, ln_g_ref, ln_b_ref, out_ref, outb_ref,
              st_s, hist_s, hb_s, q_s, c_s, b_s, v_s, og_s, ob_s, sh_s, gate_s, hv_s, *, alpha):
    D = hv_s.shape[-1]
    H = HG_HEADS
    HALF = TL // 2
    hb_s[...] = hv_s[...].astype(BF16)

    def proj(g, c0=0, width=None):
        lo = g * D + c0
        return _dot(hb_s[...], w_in_ref[:, lo:lo + (D if width is None else width)])


    for c0 in range(0, D, PROJ_COLS):
        cols = slice(c0, c0 + PROJ_COLS)

        def split_heads(dst, val):
            for i in range(PROJ_COLS // LANES):
                dst[c0 // LANES + i] = val[:, i * LANES:(i + 1) * LANES]

        p = proj(0, c0, PROJ_COLS)
        split_heads(q_s, p * (jnp.tanh(p) + 1.0))

        z = proj(1, c0, PROJ_COLS)
        lb = lb_ref[:, cols]
        one_m_lb = 1.0 - lb
        e = jnp.exp2(jnp.abs(z) * (-LOG2E))
        log_sz = jnp.minimum(z, 0.0) - jnp.log(1.0 + e)
        f = lb + one_m_lb * jnp.exp(log_sz)
        log_f = jnp.where(lb > 0.0, jnp.log(f), log_sz)
        log_k = (log_sz - z) + jnp.log(one_m_lb)
        bq = _cumsum_rows(log_f) * LOG2E
        split_heads(b_s, bq)
        split_heads(c_s, log_k * LOG2E - bq)

        split_heads(v_s, proj(2, c0, PROJ_COLS))
        p = proj(3, c0, PROJ_COLS)
        split_heads(og_s, p * (jnp.tanh(p) + 1.0))
        hist_s[HIST:HIST + TL, cols] = (proj(4, c0, PROJ_COLS)
                                        * (jnp.tanh(proj(5, c0, PROJ_COLS)) + 1.0))

    ri = lax.broadcasted_iota(jnp.int32, (HALF, HALF), 0)
    ci = lax.broadcasted_iota(jnp.int32, (HALF, HALF), 1)
    lane8 = lax.broadcasted_iota(jnp.int32, (SUBLANES, LANES), 1)
    sub8 = lax.broadcasted_iota(jnp.int32, (SUBLANES, LANES), 0)

    def head_body(hd, carry):
        q = q_s[hd]
        bc = b_s[hd]
        cc = c_s[hd]
        vb = v_s[hd].astype(BF16)
        st = st_s[hd]
        b_last = bc[TL - 1:TL, :]

        o_inter = _dot_nt((q * jnp.exp2(bc)).astype(BF16), st.astype(BF16))
        k_end = jnp.exp2(b_last + cc).astype(BF16)
        st_s[hd] = st * jnp.exp2(b_last) + _dot_tn(vb, k_end)

        def diag_block(r0):
            q8, b8 = q[r0:r0 + SUBLANES], bc[r0:r0 + SUBLANES]
            c0 = r0 % HALF
            acc = jnp.zeros((SUBLANES, LANES), F32)
            for s in range(SUBLANES):
                w = jnp.exp2(b8 + c_s[hd, r0 + s:r0 + s + 1, :])
                col = jnp.sum(q8 * w, axis=-1, keepdims=True)
                acc = jnp.where(lane8 == c0 + s, col, acc)
            keep = (lane8 >= c0) & (lane8 - c0 <= sub8)
            return jnp.where(keep, acc, 0.0)

        b_mid = bc[HALF - 1:HALF, :]
        a10 = _dot_nt((q[HALF:] * jnp.exp2(bc[HALF:] - b_mid)).astype(BF16),
                      jnp.exp2(b_mid + cc[:HALF]).astype(BF16))
        atts = []
        for hh in range(2):
            sl = slice(hh * HALF, (hh + 1) * HALF)
            qh, ch, bh = q[sl], cc[sl], bc[sl]
            att = jnp.concatenate(
                [diag_block(hh * HALF + i * SUBLANES) for i in range(HALF // SUBLANES)], axis=0)
            m = HALF // 2
            while m >= SUBLANES:
                sib = (ri & -(2 * m)) == (ci & -(2 * m))
                att = att + jnp.where(sib, _level_scores(qh, ch, bh, m), 0.0)
                m //= 2
            atts.append(att)
        o0 = _dot(atts[0].astype(BF16), vb[:HALF])
        o1 = _dot(jnp.concatenate([a10, atts[1]], axis=1).astype(BF16), vb)
        o = o_inter + jnp.concatenate([o0, o1], axis=0)

        ms = jnp.mean(o * o, axis=-1, keepdims=True)
        ob_s[hd] = (o * lax.rsqrt(ms + RMS_EPS) * hgg_ref[hd] * og_s[hd]).astype(BF16)
        return carry

    per_gate = D // PROJ_COLS
    for hd in range(H):
        head_body(hd, 0)
        for unit in range(hd * 2 * per_gate // H, (hd + 1) * 2 * per_gate // H):
            which, c0 = unit // per_gate, (unit % per_gate) * PROJ_COLS
            gate_s[which, :, c0:c0 + PROJ_COLS] = jnp.tanh(proj(6 + which, c0, PROJ_COLS)) + 1.0

    y_rec = _dot(jnp.concatenate([ob_s[hd] for hd in range(H)], axis=1), w_hg_ref[...])

    base = HIST - (CONV_WIDTH - 1)
    nshift = HIST + TL - SUBLANES
    strips = []
    for c in range(D // LANES):
        cs = slice(c * LANES, (c + 1) * LANES)
        for r in range(1, SUBLANES):
            sh_s[r - 1] = hist_s[r:r + nshift, cs]
        chunks = []
        for rc in range(0, TL, CONV_ROWS):
            acc = jnp.broadcast_to(cb_ref[:, cs], (CONV_ROWS, LANES))
            for t in range(CONV_WIDTH):
                a, r = divmod(base + t, SUBLANES)
                lo = a * SUBLANES + rc
                src = hist_s[lo:lo + CONV_ROWS, cs] if r == 0 else sh_s[r - 1, lo:lo + CONV_ROWS, :]
                acc = acc + src * cw_ref[t:t + 1, cs]
            chunks.append(acc)
        strips.append(jnp.concatenate(chunks, axis=0))
    yc = jnp.concatenate(strips, axis=1)
    hist_s[0:HIST, :] = hist_s[TL:TL + HIST, :]
    yc = _layer_norm(yc, clg_ref[...], clb_ref[...])
    y_conv = _dot((yc * (jnp.tanh(yc) + 1.0)).astype(BF16), w_cv_ref[...])

    mixed = gate_s[0] * y_rec + gate_s[1] * y_conv
    res = alpha * hv_s[...] + _dot(mixed.astype(BF16), w_out_ref[...])
    hn = _layer_norm(res, ln_g_ref[...], ln_b_ref[...])
    out_ref[...] = hn
    outb_ref[...] = hn.astype(BF16)


def _mix_layer(tile_in, w_in, lb, hg_g, w_hg, conv_w, conv_b, cln_g, cln_b, w_cv, w_out, ln_g,
               ln_b, *, alpha):
    row = lambda a: a.reshape(1, -1).astype(F32)
    fuse_ln_in = len(tile_in) == 4
    if fuse_ln_in:
        x, meta_tokens, g_in, b_in = tile_in
        B, S, D = x.shape
        Lp = S + TL
        meta_pad = jnp.pad(meta_tokens.astype(F32), ((TL - N_META, 0), (0, 0)))
        tile_args = (x, meta_pad, row(g_in), row(b_in))
        tile_specs = [
            pl.BlockSpec((None, TL, D), lambda bi, j: (bi, jnp.maximum(j - 1, 0), 0)),
            _const_spec((TL, D)), _const_spec((1, D)), _const_spec((1, D))]
    else:
        B, Lp, D = tile_in[0].shape
        tile_args = tile_in
        tile_specs = [pl.BlockSpec((None, TL, D), lambda bi, j: (bi, j, 0))]
    H = HG_HEADS
    nt = Lp // TL
    col_half = jnp.concatenate([
        jnp.full((D,), 0.5, F32), jnp.ones((2 * D,), F32), jnp.full((5 * D,), 0.5, F32)])
    w_in = w_in * col_half
    cln_g, cln_b, w_out = 0.5 * cln_g, 0.5 * cln_b, 0.5 * w_out
    return pl.pallas_call(
        functools.partial(_mix_kernel, pad=TL - N_META, alpha=alpha, fuse_ln_in=fuse_ln_in),
        grid=(B, nt),
        in_specs=tile_specs + [
            _const_spec(w_in.shape),
            _const_spec((1, D)),
            _const_spec((H, 1, LANES)),
            _const_spec((D, D)),
            _const_spec((CONV_WIDTH, D)),
            _const_spec((1, D)),
            _const_spec((1, D)),
            _const_spec((1, D)),
            _const_spec((D, D)),
            _const_spec((D, D)),
            _const_spec((1, D)),
            _const_spec((1, D)),
        ],
        out_specs=[pl.BlockSpec((None, TL, D), lambda bi, j: (bi, j, 0)),
                   pl.BlockSpec((None, TL, D), lambda bi, j: (bi, j, 0))],
        out_shape=[jax.ShapeDtypeStruct((B, Lp, D), F32),
                   jax.ShapeDtypeStruct((B, Lp, D), BF16)],
        scratch_shapes=[
            pltpu.VMEM((H, HG_DK, HG_DK), F32),
            pltpu.VMEM((HIST + TL, D), F32),
            pltpu.VMEM((TL, D), BF16),
            pltpu.VMEM((H, TL, LANES), F32),
            pltpu.VMEM((H, TL, LANES), F32),
            pltpu.VMEM((H, TL, LANES), F32),
            pltpu.VMEM((H, TL, LANES), F32),
            pltpu.VMEM((H, TL, LANES), F32),
            pltpu.VMEM((H, TL, LANES), BF16),
            pltpu.VMEM((SUBLANES - 1, HIST + TL - SUBLANES, LANES), F32),
            pltpu.VMEM((2, TL, D), F32),
            pltpu.VMEM((TL, D), F32),
            pltpu.VMEM((H, HG_DK, HG_DK), F32),
            pltpu.VMEM((HIST, D), F32),
        ],
        compiler_params=pltpu.CompilerParams(
            dimension_semantics=("arbitrary", "arbitrary"),
            vmem_limit_bytes=VMEM_LIMIT),
        name="mix_layer",
    )(*tile_args, w_in.astype(BF16), row(lb), hg_g.reshape(H, 1, LANES).astype(F32), w_hg.astype(BF16),
      conv_w.astype(F32), row(conv_b), row(cln_g), row(cln_b), w_cv.astype(BF16),
      w_out.astype(BF16), row(ln_g), row(ln_b))


def _ffn_kernel(h_ref, hb_ref, wg_ref, wu_ref, wd_ref, g_ref, b_ref, out_ref, a_s, *, alpha, fc,
                tiles_per_seq):
    i = pl.program_id(0)
    dead = (i % tiles_per_seq == 0) & (i > 0)

    @pl.when(dead)
    def _():
        out_ref[...] = jnp.zeros_like(out_ref)

    @pl.when(jnp.logical_not(dead))
    def _():
        xb = hb_ref[...]
        F = wg_ref.shape[1]
        for c in range(F // fc):
            cs = slice(c * fc, (c + 1) * fc)
            g = _dot(xb, wg_ref[:, cs])
            u = _dot(xb, wu_ref[:, cs])
            a_s[:, cs] = (g * _sigmoid(g) * u).astype(BF16)
        res = alpha * h_ref[...] + _dot(a_s[...], wd_ref[...])
        out_ref[...] = _layer_norm(res, g_ref[...], b_ref[...])


def _ffn_chunk(F):
    for fc in (256, 128):
        if F % fc == 0:
            return fc
    return F


def _dense_ffn(h, hb, wg, wu, wd, ln_g, ln_b, *, alpha):
    B, Lp, D = h.shape
    F = wg.shape[1]
    R = B * Lp
    out = pl.pallas_call(
        functools.partial(_ffn_kernel, alpha=alpha, fc=_ffn_chunk(F), tiles_per_seq=Lp // TL),
        grid=(R // TL,),
        in_specs=[
            pl.BlockSpec((TL, D), lambda i: (i, 0)),
            pl.BlockSpec((TL, D), lambda i: (i, 0)),
            _const_spec((D, F)),
            _const_spec((D, F)),
            _const_spec((F, D)),
            _const_spec((1, D)),
            _const_spec((1, D)),
        ],
        out_specs=pl.BlockSpec((TL, D), lambda i: (i, 0)),
        out_shape=jax.ShapeDtypeStruct((R, D), F32),
        scratch_shapes=[pltpu.VMEM((TL, F), BF16)],
        compiler_params=pltpu.CompilerParams(
            dimension_semantics=("arbitrary",), vmem_limit_bytes=VMEM_LIMIT),
        name="dense_ffn",
    )(h.reshape(R, D), hb.reshape(R, D), wg.astype(BF16), wu.astype(BF16), wd.astype(BF16),
      ln_g.reshape(1, D).astype(F32), ln_b.reshape(1, D).astype(F32))
    return out.reshape(B, Lp, D)


def _router_kernel(h_ref, wr_ref, gate_ref, rank_ref, cnt_s):
    first = (pl.program_id(0) == 0) & (pl.program_id(1) == 0)

    @pl.when(first)
    def _():
        cnt_s[...] = jnp.zeros_like(cnt_s)

    hv, wv = h_ref[...], wr_ref[...]
    h_hi, w_hi = hv.astype(BF16), wv.astype(BF16)
    h_lo = (hv - h_hi.astype(F32)).astype(BF16)
    w_lo = (wv - w_hi.astype(F32)).astype(BF16)
    both = _dot(h_hi, jnp.concatenate([w_hi, w_lo], axis=1))
    logits = both[:, :LANES] + (both[:, LANES:] + _dot(h_lo, w_hi))
    lane = lax.broadcasted_iota(jnp.int32, (TL, LANES), 1).astype(F32)
    neg = jnp.float32(-jnp.inf)
    logits = jnp.where(lane < N_EXPERTS, logits, neg)
    m1 = jnp.max(logits, axis=-1, keepdims=True)
    i1 = jnp.min(jnp.where(logits == m1, lane, float(LANES)), axis=-1, keepdims=True)
    oh1 = lane == i1
    rest = jnp.where(oh1, neg, logits)
    m2 = jnp.max(rest, axis=-1, keepdims=True)
    i2 = jnp.min(jnp.where(rest == m2, lane, float(LANES)), axis=-1, keepdims=True)
    oh2 = lane == i2
    e = jnp.exp(m2 - m1)
    w1 = 1.0 / (1.0 + e)
    w2 = e * w1
    gates = jnp.where(oh1, w1, 0.0) + jnp.where(oh2, w2, 0.0)
    gate_ref[...] = gates
    sel = (gates > 0.0).astype(BF16)
    ri = lax.broadcasted_iota(jnp.int32, (TL, TL), 0)
    ci = lax.broadcasted_iota(jnp.int32, (TL, TL), 1)
    before = _dot((ci < ri).astype(BF16), sel)
    cnt = cnt_s[...]
    rank_ref[...] = (before + cnt).astype(jnp.int32)
    cnt_s[...] = cnt + before[TL - 1:TL, :] + sel[TL - 1:TL, :].astype(F32)


def _router(h, w_router, seq):
    B, Lp, D = h.shape
    nj = seq // TL
    T = B * seq
    wr = jnp.pad(w_router.astype(F32), ((0, 0), (0, LANES - N_EXPERTS)))
    return pl.pallas_call(
        _router_kernel,
        grid=(B, nj),
        in_specs=[
            pl.BlockSpec((None, TL, D), lambda bi, j: (bi, j + 1, 0)),
            _const_spec((D, LANES)),
        ],
        out_specs=[
            pl.BlockSpec((TL, LANES), lambda bi, j: (bi * nj + j, 0)),
            pl.BlockSpec((TL, LANES), lambda bi, j: (bi * nj + j, 0)),
        ],
        out_shape=[jax.ShapeDtypeStruct((T, LANES), F32),
                   jax.ShapeDtypeStruct((T, LANES), jnp.int32)],
        scratch_shapes=[pltpu.VMEM((1, LANES), F32)],
        compiler_params=pltpu.CompilerParams(
            dimension_semantics=("arbitrary", "arbitrary")),
        name="moe_router",
    )(h, wr)


def _gather_steps(lo_be, hi_be, counts, tstart, texp, n_used, n_steps):
    nb, ne = lo_be.shape
    nt = texp.shape[0]
    tiles = jnp.arange(nt, dtype=jnp.int32)
    of_tile = texp[:, None] == jnp.arange(ne, dtype=jnp.int32)[None, :]
    pick_e = lambda a: jnp.sum(jnp.where(of_tile[:, :, None], a.T[None, :, :], 0), axis=1)
    pick_v = lambda v: jnp.sum(jnp.where(of_tile, v[None, :], 0), axis=1)
    s0 = (tiles - pick_v(tstart)) * TL
    s1 = jnp.minimum(s0 + TL, pick_v(counts))
    ov = (pick_e(lo_be) < s1[:, None]) & (pick_e(hi_be) > s0[:, None]) & (tiles < n_used)[:, None]
    nblk = jnp.sum(ov.astype(jnp.int32), axis=1)
    blk_ids = jnp.arange(nb, dtype=jnp.int32)
    b_first = jnp.min(jnp.where(ov, blk_ids[None, :], nb), axis=1)
    nstep = (nblk + GATHER_BLOCKS - 1) // GATHER_BLOCKS
    end = jnp.cumsum(nstep)
    total = end[-1]
    start = end - nstep
    s = jnp.arange(n_steps, dtype=jnp.int32)
    sc = jnp.minimum(s, total - 1)
    tile = jnp.sum((end[None, :] <= sc[:, None]).astype(jnp.int32), axis=1)
    hit = tile[:, None] == tiles[None, :]
    pick_t = lambda a: jnp.sum(jnp.where(hit, a[None, :], 0), axis=1)
    u = sc - pick_t(start)
    valid = s < total
    b0 = pick_t(b_first) + GATHER_BLOCKS * u
    b_last = pick_t(b_first + nblk - 1)
    nv = jnp.where(valid, jnp.minimum(GATHER_BLOCKS, pick_t(nblk) - GATHER_BLOCKS * u), 0)
    first = valid & (u == 0)
    last = valid & (u == pick_t(nstep) - 1)
    i32 = lambda a: a.astype(jnp.int32)
    return i32(tile), i32(b0), i32(b_last), i32(nv), i32(first), i32(last)


def _expert_up_kernel(tile_ref, b0_ref, bl_ref, nv_ref, first_ref, last_ref, texp_ref, *refs, fc):
    nk = GATHER_BLOCKS
    h_refs, route_refs = refs[:nk], refs[nk:2 * nk]
    wg_ref, wu_ref, a_ref, gs_ref, x_s, g_s = refs[2 * nk:]
    p = pl.program_id(0)
    tile = tile_ref[p]
    e = texp_ref[tile]

    @pl.when(first_ref[p] == 1)
    def _():
        x_s[...] = jnp.zeros_like(x_s)
        g_s[...] = jnp.zeros_like(g_s)

    for k in range(nk):
        @pl.when(k < nv_ref[p])
        def _(k=k):
            srow = route_refs[k][pl.ds(e, 1), :] - tile * TL
            grow = pltpu.bitcast(route_refs[k][pl.ds(N_EXPERTS + e, 1), :], F32)
            ri = lax.broadcasted_iota(jnp.int32, (TL, TL), 0)
            hit = ri == srow
            x_s[...] += _dot(hit.astype(BF16), h_refs[k][...]).astype(BF16)
            g_s[...] += jnp.sum(jnp.where(hit, grow, 0.0), axis=-1, keepdims=True)

    @pl.when(last_ref[p] == 1)
    def _():
        xb = x_s[...]
        F = wg_ref.shape[1]
        for c in range(F // fc):
            cs = slice(c * fc, (c + 1) * fc)
            g = _dot(xb, wg_ref[:, cs])
            u = _dot(xb, wu_ref[:, cs])
            a_ref[:, cs] = (g * _sigmoid(g) * u).astype(BF16)
        gs_ref[...] = g_s[...]


def _expert_down_kernel(texp_ref, nused_ref, a_ref, gs_ref, wd_ref, y_ref):
    i = pl.program_id(0)

    @pl.when(i < nused_ref[0])
    def _():
        y_ref[...] = (_dot(a_ref[...], wd_ref[...]) * gs_ref[:, 0:1]).astype(BF16)


def _combine_kernel(cidx_ref, cval_ref, *refs, alpha):
    npair = 2 * N_EXPERTS
    y_refs = refs[:npair]
    slot_ref, h_ref, g_ref, b_ref, out_ref, acc_s = refs[npair:]
    b = pl.program_id(0)
    ci = lax.broadcasted_iota(jnp.int32, (TL, TL), 1)

    def contrib(k):
        e = k // 2
        scol = slot_ref[:, e:e + 1] - cidx_ref[b * npair + k] * TL
        return _dot((ci == scol).astype(BF16), y_refs[k][...])

    acc = contrib(0)
    for e in range(1, N_EXPERTS):
        acc = acc + contrib(2 * e)
    acc_s[...] = acc
    for e in range(N_EXPERTS):
        @pl.when(cval_ref[b * N_EXPERTS + e] == 1)
        def _(e=e):
            acc_s[...] += contrib(2 * e + 1)

    res = alpha * h_ref[...] + acc_s[...]
    out_ref[...] = _layer_norm(res, g_ref[...], b_ref[...])


def _moe_layer(h, hb, w_router, wg, wu, wd, ln_g, ln_b, seq, *, alpha):
    B, Lp, D = h.shape
    E = N_EXPERTS
    F = wg.shape[-1]
    nj = seq // TL
    T = B * seq
    NB = T // TL
    NT = (2 * T) // TL + E
    NS = -(-(NB * E) // GATHER_BLOCKS) + NT

    gate, rank = _router(h, w_router, seq)

    gate8 = gate[:, :E]
    rank8 = rank[:, :E]
    sel8 = gate8 > 0.0
    counts = rank8[-1] + sel8[-1].astype(jnp.int32)
    ntile = (counts + TL - 1) // TL
    tstart = jnp.cumsum(ntile) - ntile
    n_used = jnp.sum(ntile).astype(jnp.int32).reshape(1)
    slot8 = jnp.where(sel8, tstart[None, :] * TL + rank8, -1).astype(jnp.int32)
    slot_tm = jnp.pad(slot8, ((0, 0), (0, LANES - E)), constant_values=-1)
    route = jnp.concatenate([slot8, lax.bitcast_convert_type(gate8, jnp.int32)], axis=1)
    route_em = route.reshape(NB, TL, 2 * E).transpose(0, 2, 1)
    tiles = jnp.arange(NT, dtype=jnp.int32)
    texp = jnp.sum((jnp.cumsum(ntile)[None, :] <= tiles[:, None]).astype(jnp.int32), axis=1)
    texp = jnp.minimum(texp, E - 1)

    cb = jnp.concatenate([rank8[::TL], counts[None, :]], axis=0)
    lo_be, hi_be = cb[:-1], cb[1:]
    blk_ids = jnp.arange(NB, dtype=jnp.int32)
    g_tile, g_b0, g_bl, g_nv, g_first, g_last = _gather_steps(
        lo_be, hi_be, counts, tstart, texp, n_used[0], NS)
    nonempty = hi_be > lo_be
    t0 = tstart[None, :] + lo_be // TL
    cross = nonempty & ((hi_be - 1) // TL > lo_be // TL)
    earlier = (blk_ids[None, :] <= blk_ids[:, None])[:, :, None]
    fill = lambda v: jnp.max(jnp.where(earlier, v[None, :, :], 0), axis=1)
    cidx = jnp.stack([fill(jnp.where(nonempty, t0, 0)), fill(jnp.where(cross, t0 + 1, 0))],
                     axis=-1).reshape(-1).astype(jnp.int32)
    cval = cross.reshape(-1).astype(jnp.int32)

    hblk = lambda blk: (blk // nj, blk % nj + 1, 0)
    src = lambda k: (lambda p, t, b0, bl, nv, f, l, x: jnp.minimum(b0[p] + k, bl[p]))
    of_step = lambda fn: (lambda p, t, b0, bl, nv, f, l, x: fn(t[p], x))
    h_spec = lambda k: pl.BlockSpec((None, TL, D), lambda *a: hblk(src(k)(*a)))
    em_spec = lambda k: pl.BlockSpec((None, 2 * E, TL), lambda *a: (src(k)(*a), 0, 0))
    w_spec = pl.BlockSpec((None, D, F), of_step(lambda t, x: (x[t], 0, 0)))
    nk = GATHER_BLOCKS

    act, gslot = pl.pallas_call(
        functools.partial(_expert_up_kernel, fc=_ffn_chunk(F)),
        grid_spec=pltpu.PrefetchScalarGridSpec(
            num_scalar_prefetch=7,
            grid=(NS,),
            in_specs=([h_spec(k) for k in range(nk)] + [em_spec(k) for k in range(nk)]
                      + [w_spec, w_spec]),
            out_specs=[pl.BlockSpec((TL, F), of_step(lambda t, x: (t, 0))),
                       pl.BlockSpec((TL, LANES), of_step(lambda t, x: (t, 0)))],
            scratch_shapes=[pltpu.VMEM((TL, D), BF16), pltpu.VMEM((TL, LANES), F32)],
        ),
        out_shape=[jax.ShapeDtypeStruct((NT * TL, F), BF16),
                   jax.ShapeDtypeStruct((NT * TL, LANES), F32)],
        compiler_params=pltpu.CompilerParams(
            dimension_semantics=("arbitrary",), vmem_limit_bytes=VMEM_LIMIT),
        name="moe_expert_up",
    )(g_tile, g_b0, g_bl, g_nv, g_first, g_last, texp, *([hb] * nk), *([route_em] * nk),
      wg.astype(BF16), wu.astype(BF16))

    used = lambda i, n: jnp.minimum(i, jnp.maximum(n[0] - 1, 0))
    ys = pl.pallas_call(
        _expert_down_kernel,
        grid_spec=pltpu.PrefetchScalarGridSpec(
            num_scalar_prefetch=2,
            grid=(NT,),
            in_specs=[
                pl.BlockSpec((TL, F), lambda i, x, n: (used(i, n), 0)),
                pl.BlockSpec((TL, LANES), lambda i, x, n: (used(i, n), 0)),
                pl.BlockSpec((None, F, D), lambda i, x, n: (x[used(i, n)], 0, 0)),
            ],
            out_specs=pl.BlockSpec((TL, D), lambda i, x, n: (used(i, n), 0)),
        ),
        out_shape=jax.ShapeDtypeStruct((NT * TL, D), BF16),
        compiler_params=pltpu.CompilerParams(
            dimension_semantics=("arbitrary",), vmem_limit_bytes=VMEM_LIMIT),
        name="moe_expert_down",
    )(texp, n_used, act, gslot, wd.astype(BF16))

    ypair = lambda k: pl.BlockSpec((TL, D), lambda b, ci, cv: (ci[b * 2 * E + k], 0))
    out = pl.pallas_call(
        functools.partial(_combine_kernel, alpha=alpha),
        grid_spec=pltpu.PrefetchScalarGridSpec(
            num_scalar_prefetch=2,
            grid=(NB,),
            in_specs=[ypair(k) for k in range(2 * E)] + [
                pl.BlockSpec((TL, LANES), lambda b, ci, cv: (b, 0)),
                pl.BlockSpec((None, TL, D), lambda b, ci, cv: hblk(b)),
                pl.BlockSpec((1, D), lambda b, ci, cv: (0, 0)),
                pl.BlockSpec((1, D), lambda b, ci, cv: (0, 0)),
            ],
            out_specs=pl.BlockSpec((None, TL, D), lambda b, ci, cv: (b // nj, b % nj, 0)),
            scratch_shapes=[pltpu.VMEM((TL, D), F32)],
        ),
        out_shape=jax.ShapeDtypeStruct((B, seq, D), F32),
        compiler_params=pltpu.CompilerParams(
            dimension_semantics=("arbitrary",), vmem_limit_bytes=VMEM_LIMIT),
        name="moe_combine",
    )(cidx, cval, *([ys] * (2 * E)), slot_tm, h,
      ln_g.reshape(1, D).astype(F32), ln_b.reshape(1, D).astype(F32))
    return out


def kernel(x, meta_tokens, ln_in_g, ln_in_b, w_in, lower_bounds, hg_norm_g, w_hg_out, conv_w,
           conv_b, conv_ln_g, conv_ln_b, w_conv_out, w_out, ln1_g, ln1_b, ffn_w_gate, ffn_w_up,
           ffn_w_down, moe_router, moe_w_gate, moe_w_up, moe_w_down, ln2_g, ln2_b):
    B, S, D = x.shape
    depth = w_in.shape[0]
    assert S % TL == 0 and D % LANES == 0 and meta_tokens.shape[0] == N_META
    alpha = float((2 * depth) ** 0.25)

    pr = jax.nn.softmax(lower_bounds.astype(F32), axis=0)
    lb_all = jnp.cumsum(pr, axis=0) - pr[0]

    tile_in = (x, meta_tokens, ln_in_g, ln_in_b)
    for l in range(depth):
        h, hb = _mix_layer(tile_in if l == 0 else (h,), w_in[l], lb_all[l], hg_norm_g[l], w_hg_out[l], conv_w[l], conv_b[l],
                           conv_ln_g[l], conv_ln_b[l], w_conv_out[l], w_out[l], ln1_g[l],
                           ln1_b[l], alpha=alpha)
        jl = l // 2
        last = l == depth - 1
        if l % 2 == 0:
            h = _dense_ffn(h, hb, ffn_w_gate[jl], ffn_w_up[jl], ffn_w_down[jl], ln2_g[l],
                           ln2_b[l], alpha=alpha)
            if last:
                h = h[:, TL:]
        else:
            assert last, "MoE layers other than the last one would need the padded layout back"
            h = _moe_layer(h, hb, moe_router[jl], moe_w_gate[jl], moe_w_up[jl], moe_w_down[jl],
                           ln2_g[l], ln2_b[l], S, alpha=alpha)
    return h
```

```python
import functools

import jax
import jax.numpy as jnp
from jax import lax
from jax.experimental import pallas as pl
from jax.experimental.pallas import tpu as pltpu

F32 = jnp.float32
BF16 = jnp.bfloat16

N_META = 16
HG_HEADS = 8
HG_DK = 128
CONV_WIDTH = 31
N_EXPERTS = 8
LN_EPS = 1e-5
RMS_EPS = 1e-6

LANES = 128
SUBLANES = 8
TL = 256
HIST = 32
CONV_ROWS = 64
GATHER_BLOCKS = 8
PROJ_COLS = 256
VMEM_LIMIT = 56 * 1024 * 1024
LOG2E = 1.4426950408889634


def _sigmoid(x):
    return 0.5 * jnp.tanh(0.5 * x) + 0.5


def _layer_norm(x, g, b):
    mu = jnp.mean(x, axis=-1, keepdims=True)
    xc = x - mu
    var = jnp.mean(xc * xc, axis=-1, keepdims=True)
    return xc * lax.rsqrt(var + LN_EPS) * g + b


def _dot(a, b):
    return jnp.dot(a, b, preferred_element_type=F32)


def _dot_nt(a, b):
    return lax.dot_general(a, b, (((1,), (1,)), ((), ())), preferred_element_type=F32)


def _dot_tn(a, b):
    return lax.dot_general(a, b, (((0,), (0,)), ((), ())), preferred_element_type=F32)


def _const_spec(shape):
    nd = len(shape)
    return pl.BlockSpec(shape, lambda *_: (0,) * nd, pipeline_mode=pl.Buffered(1))


def _cumsum_rows(x):
    n = x.shape[0]
    hi = x.astype(BF16)
    r1 = x - hi.astype(F32)
    mid = r1.astype(BF16)
    lo = (r1 - mid.astype(F32)).astype(BF16)
    ri = lax.broadcasted_iota(jnp.int32, (n, n), 0)
    ci = lax.broadcasted_iota(jnp.int32, (n, n), 1)
    tri = (ri >= ci).astype(BF16)
    return _dot(tri, hi) + _dot(tri, mid) + _dot(tri, lo)


def _level_scores(qh, ch, bh, m):
    n = qh.shape[0]
    zero = jnp.zeros((m, qh.shape[1]), F32)
    qp, kp = [], []
    for jb in range(n // (2 * m)):
        lo, mid, hi = jb * 2 * m, jb * 2 * m + m, (jb + 1) * 2 * m
        e = bh[mid - 1:mid, :]
        kp += [jnp.exp2(e + ch[lo:mid]), zero]
        qp += [zero, qh[mid:hi] * jnp.exp2(bh[mid:hi] - e)]
    qp = jnp.concatenate(qp, axis=0).astype(BF16)
    kp = jnp.concatenate(kp, axis=0).astype(BF16)
    return _dot_nt(qp, kp)


def _mix_kernel(*refs, pad, alpha, fuse_ln_in):
    n_in = 4 if fuse_ln_in else 1
    tile_in, refs = refs[:n_in], refs[n_in:]
    out_ref, outb_ref = refs[12], refs[13]
    st_s, hist_s = refs[14], refs[15]
    hv_s, stm_s, histm_s = refs[-3:]
    b = pl.program_id(0)
    j = pl.program_id(1)
    first = j == 0

    @pl.when(first & (b == 0))
    def _():
        st_s[...] = jnp.zeros_like(st_s)
        hist_s[0:HIST, :] = jnp.zeros((HIST, hist_s.shape[1]), F32)

    @pl.when(first & (b > 0))
    def _():
        st_s[...] = stm_s[...]
        hist_s[0:HIST, :] = histm_s[...]
        out_ref[...] = jnp.zeros_like(out_ref)
        outb_ref[...] = jnp.zeros_like(outb_ref)

    @pl.when(jnp.logical_not(first) | (b == 0))
    def _():
        rows = lax.broadcasted_iota(jnp.int32, (TL, 1), 0)
        if fuse_ln_in:
            x_ref, meta_ref, g_ref, b_ref = tile_in

            @pl.when(first)
            def _():
                y = _layer_norm(meta_ref[...], g_ref[...], b_ref[...])
                hv_s[...] = jnp.where(rows >= pad, y, 0.0)

            @pl.when(jnp.logical_not(first))
            def _():
                hv_s[...] = _layer_norm(x_ref[...], g_ref[...], b_ref[...])
        else:
            hv_s[...] = jnp.where(j * TL + rows >= pad, tile_in[0][...], 0.0)

        _mix_tile(*refs[:-2], alpha=alpha)

        @pl.when(first)
        def _():
            stm_s[...] = st_s[...]
            histm_s[...] = hist_s[0:HIST, :]


def _mix_tile(w_in_ref, lb_ref, hgg_ref, w_hg_ref, cw_ref, cb_ref, clg_ref, clb_ref,
              w_cv_ref, w_out_ref, ln_g_ref, ln_b_ref, out_ref, outb_ref,
              st_s, hist_s, hb_s, q_s, c_s, b_s, v_s, og_s, ob_s, sh_s, gate_s, hv_s, *, alpha):
    D = hv_s.shape[-1]
    H = HG_HEADS
    HALF = TL // 2
    hb_s[...] = hv_s[...].astype(BF16)

    def proj(g, c0=0, width=None):
        lo = g * D + c0
        return _dot(hb_s[...], w_in_ref[:, lo:lo + (D if width is None else width)])


    for c0 in range(0, D, PROJ_COLS):
        cols = slice(c0, c0 + PROJ_COLS)

        def split_heads(dst, val):
            for i in range(PROJ_COLS // LANES):
                dst[c0 // LANES + i] = val[:, i * LANES:(i + 1) * LANES]

        p = proj(0, c0, PROJ_COLS)
        split_heads(q_s, p * (jnp.tanh(p) + 1.0))

        z = proj(1, c0, PROJ_COLS)
        lb = lb_ref[:, cols]
        one_m_lb = 1.0 - lb
        e = jnp.exp2(jnp.abs(z) * (-LOG2E))
        log_sz = jnp.minimum(z, 0.0) - jnp.log(1.0 + e)
        f = lb + one_m_lb * jnp.exp(log_sz)
        log_f = jnp.where(lb > 0.0, jnp.log(f), log_sz)
        log_k = (log_sz - z) + jnp.log(one_m_lb)
        bq = _cumsum_rows(log_f) * LOG2E
        split_heads(b_s, bq)
        split_heads(c_s, log_k * LOG2E - bq)

        split_heads(v_s, proj(2, c0, PROJ_COLS))
        p = proj(3, c0, PROJ_COLS)
        split_heads(og_s, p * (jnp.tanh(p) + 1.0))
        hist_s[HIST:HIST + TL, cols] = (proj(4, c0, PROJ_COLS)
                                        * (jnp.tanh(proj(5, c0, PROJ_COLS)) + 1.0))

    ri = lax.broadcasted_iota(jnp.int32, (HALF, HALF), 0)
    ci = lax.broadcasted_iota(jnp.int32, (HALF, HALF), 1)
    lane8 = lax.broadcasted_iota(jnp.int32, (SUBLANES, LANES), 1)
    sub8 = lax.broadcasted_iota(jnp.int32, (SUBLANES, LANES), 0)

    def head_body(hd, carry):
        q = q_s[hd]
        bc = b_s[hd]
        cc = c_s[hd]
        vb = v_s[hd].astype(BF16)
        st = st_s[hd]
        b_last = bc[TL - 1:TL, :]

        o_inter = _dot_nt((q * jnp.exp2(bc)).astype(BF16), st.astype(BF16))
        k_end = jnp.exp2(b_last + cc).astype(BF16)
        st_s[hd] = st * jnp.exp2(b_last) + _dot_tn(vb, k_end)

        def diag_block(r0):
            q8, b8 = q[r0:r0 + SUBLANES], bc[r0:r0 + SUBLANES]
            c0 = r0 % HALF
            acc = jnp.zeros((SUBLANES, LANES), F32)
            for s in range(SUBLANES):
                w = jnp.exp2(b8 + c_s[hd, r0 + s:r0 + s + 1, :])
                col = jnp.sum(q8 * w, axis=-1, keepdims=True)
                acc = jnp.where(lane8 == c0 + s, col, acc)
            keep = (lane8 >= c0) & (lane8 - c0 <= sub8)
            return jnp.where(keep, acc, 0.0)

        b_mid = bc[HALF - 1:HALF, :]
        a10 = _dot_nt((q[HALF:] * jnp.exp2(bc[HALF:] - b_mid)).astype(BF16),
                      jnp.exp2(b_mid + cc[:HALF]).astype(BF16))
        atts = []
        for hh in range(2):
            sl = slice(hh * HALF, (hh + 1) * HALF)
            qh, ch, bh = q[sl], cc[sl], bc[sl]
            att = jnp.concatenate(
                [diag_block(hh * HALF + i * SUBLANES) for i in range(HALF // SUBLANES)], axis=0)
            m = HALF // 2
            while m >= SUBLANES:
                sib = (ri & -(2 * m)) == (ci & -(2 * m))
                att = att + jnp.where(sib, _level_scores(qh, ch, bh, m), 0.0)
                m //= 2
            atts.append(att)
        o0 = _dot(atts[0].astype(BF16), vb[:HALF])
        o1 = _dot(jnp.concatenate([a10, atts[1]], axis=1).astype(BF16), vb)
        o = o_inter + jnp.concatenate([o0, o1], axis=0)

        ms = jnp.mean(o * o, axis=-1, keepdims=True)
        ob_s[hd] = (o * lax.rsqrt(ms + RMS_EPS) * hgg_ref[hd] * og_s[hd]).astype(BF16)
        return carry

    per_gate = D // PROJ_COLS
    for hd in range(H):
        head_body(hd, 0)
        for unit in range(hd * 2 * per_gate // H, (hd + 1) * 2 * per_gate // H):
            which, c0 = unit // per_gate, (unit % per_gate) * PROJ_COLS
            gate_s[which, :, c0:c0 + PROJ_COLS] = jnp.tanh(proj(6 + which, c0, PROJ_COLS)) + 1.0

    y_rec = _dot(jnp.concatenate([ob_s[hd] for hd in range(H)], axis=1), w_hg_ref[...])

    base = HIST - (CONV_WIDTH - 1)
    nshift = HIST + TL - SUBLANES
    strips = []
    for c in range(D // LANES):
        cs = slice(c * LANES, (c + 1) * LANES)
        for r in range(1, SUBLANES):
            sh_s[r - 1] = hist_s[r:r + nshift, cs]
        chunks = []
        for rc in range(0, TL, CONV_ROWS):
            acc = jnp.broadcast_to(cb_ref[:, cs], (CONV_ROWS, LANES))
            for t in range(CONV_WIDTH):
                a, r = divmod(base + t, SUBLANES)
                lo = a * SUBLANES + rc
                src = hist_s[lo:lo + CONV_ROWS, cs] if r == 0 else sh_s[r - 1, lo:lo + CONV_ROWS, :]
                acc = acc + src * cw_ref[t:t + 1, cs]
            chunks.append(acc)
        strips.append(jnp.concatenate(chunks, axis=0))
    yc = jnp.concatenate(strips, axis=1)
    hist_s[0:HIST, :] = hist_s[TL:TL + HIST, :]
    yc = _layer_norm(yc, clg_ref[...], clb_ref[...])
    y_conv = _dot((yc * (jnp.tanh(yc) + 1.0)).astype(BF16), w_cv_ref[...])

    mixed = gate_s[0] * y_rec + gate_s[1] * y_conv
    res = alpha * hv_s[...] + _dot(mixed.astype(BF16), w_out_ref[...])
    hn = _layer_norm(res, ln_g_ref[...], ln_b_ref[...])
    out_ref[...] = hn
    outb_ref[...] = hn.astype(BF16)


def _mix_layer(tile_in, w_in, lb, hg_g, w_hg, conv_w, conv_b, cln_g, cln_b, w_cv, w_out, ln_g,
               ln_b, *, alpha):
    row = lambda a: a.reshape(1, -1).astype(F32)
    fuse_ln_in = len(tile_in) == 4
    if fuse_ln_in:
        x, meta_tokens, g_in, b_in = tile_in
        B, S, D = x.shape
        Lp = S + TL
        meta_pad = jnp.pad(meta_tokens.astype(F32), ((TL - N_META, 0), (0, 0)))
        tile_args = (x, meta_pad, row(g_in), row(b_in))
        tile_specs = [
            pl.BlockSpec((None, TL, D), lambda bi, j: (bi, jnp.maximum(j - 1, 0), 0)),
            _const_spec((TL, D)), _const_spec((1, D)), _const_spec((1, D))]
    else:
        B, Lp, D = tile_in[0].shape
        tile_args = tile_in
        tile_specs = [pl.BlockSpec((None, TL, D), lambda bi, j: (bi, j, 0))]
    H = HG_HEADS
    nt = Lp // TL
    col_half = jnp.concatenate([
        jnp.full((D,), 0.5, F32), jnp.ones((2 * D,), F32), jnp.full((5 * D,), 0.5, F32)])
    w_in = w_in * col_half
    cln_g, cln_b, w_out = 0.5 * cln_g, 0.5 * cln_b, 0.5 * w_out
    return pl.pallas_call(
        functools.partial(_mix_kernel, pad=TL - N_META, alpha=alpha, fuse_ln_in=fuse_ln_in),
        grid=(B, nt),
        in_specs=tile_specs + [
            _const_spec(w_in.shape),
            _const_spec((1, D)),
            _const_spec((H, 1, LANES)),
            _const_spec((D, D)),
            _const_spec((CONV_WIDTH, D)),
            _const_spec((1, D)),
            _const_spec((1, D)),
            _const_spec((1, D)),
            _const_spec((D, D)),
            _const_spec((D, D)),
            _const_spec((1, D)),
            _const_spec((1, D)),
        ],
        out_specs=[pl.BlockSpec((None, TL, D), lambda bi, j: (bi, j, 0)),
                   pl.BlockSpec((None, TL, D), lambda bi, j: (bi, j, 0))],
        out_shape=[jax.ShapeDtypeStruct((B, Lp, D), F32),
                   jax.ShapeDtypeStruct((B, Lp, D), BF16)],
        scratch_shapes=[
            pltpu.VMEM((H, HG_DK, HG_DK), F32),
            pltpu.VMEM((HIST + TL, D), F32),
            pltpu.VMEM((TL, D), BF16),
            pltpu.VMEM((H, TL, LANES), F32),
            pltpu.VMEM((H, TL, LANES), F32),
            pltpu.VMEM((H, TL, LANES), F32),
            pltpu.VMEM((H, TL, LANES), F32),
            pltpu.VMEM((H, TL, LANES), F32),
            pltpu.VMEM((H, TL, LANES), BF16),
            pltpu.VMEM((SUBLANES - 1, HIST + TL - SUBLANES, LANES), F32),
            pltpu.VMEM((2, TL, D), F32),
            pltpu.VMEM((TL, D), F32),
            pltpu.VMEM((H, HG_DK, HG_DK), F32),
            pltpu.VMEM((HIST, D), F32),
        ],
        compiler_params=pltpu.CompilerParams(
            dimension_semantics=("arbitrary", "arbitrary"),
            vmem_limit_bytes=VMEM_LIMIT),
        name="mix_layer",
    )(*tile_args, w_in.astype(BF16), row(lb), hg_g.reshape(H, 1, LANES).astype(F32), w_hg.astype(BF16),
      conv_w.astype(F32), row(conv_b), row(cln_g), row(cln_b), w_cv.astype(BF16),
      w_out.astype(BF16), row(ln_g), row(ln_b))


def _ffn_kernel(h_ref, hb_ref, wg_ref, wu_ref, wd_ref, g_ref, b_ref, out_ref, a_s, *, alpha, fc,
                tiles_per_seq):
    i = pl.program_id(0)
    dead = (i % tiles_per_seq == 0) & (i > 0)

    @pl.when(dead)
    def _():
        out_ref[...] = jnp.zeros_like(out_ref)

    @pl.when(jnp.logical_not(dead))
    def _():
        xb = hb_ref[...]
        F = wg_ref.shape[1]
        for c in range(F // fc):
            cs = slice(c * fc, (c + 1) * fc)
            g = _dot(xb, wg_ref[:, cs])
            u = _dot(xb, wu_ref[:, cs])
            a_s[:, cs] = (g * _sigmoid(g) * u).astype(BF16)
        res = alpha * h_ref[...] + _dot(a_s[...], wd_ref[...])
        out_ref[...] = _layer_norm(res, g_ref[...], b_ref[...])


def _ffn_chunk(F):
    for fc in (256, 128):
        if F % fc == 0:
            return fc
    return F


def _dense_ffn(h, hb, wg, wu, wd, ln_g, ln_b, *, alpha):
    B, Lp, D = h.shape
    F = wg.shape[1]
    R = B * Lp
    out = pl.pallas_call(
        functools.partial(_ffn_kernel, alpha=alpha, fc=_ffn_chunk(F), tiles_per_seq=Lp // TL),
        grid=(R // TL,),
        in_specs=[
            pl.BlockSpec((TL, D), lambda i: (i, 0)),
            pl.BlockSpec((TL, D), lambda i: (i, 0)),
            _const_spec((D, F)),
            _const_spec((D, F)),
            _const_spec((F, D)),
            _const_spec((1, D)),
            _const_spec((1, D)),
        ],
        out_specs=pl.BlockSpec((TL, D), lambda i: (i, 0)),
        out_shape=jax.ShapeDtypeStruct((R, D), F32),
        scratch_shapes=[pltpu.VMEM((TL, F), BF16)],
        compiler_params=pltpu.CompilerParams(
            dimension_semantics=("arbitrary",), vmem_limit_bytes=VMEM_LIMIT),
        name="dense_ffn",
    )(h.reshape(R, D), hb.reshape(R, D), wg.astype(BF16), wu.astype(BF16), wd.astype(BF16),
      ln_g.reshape(1, D).astype(F32), ln_b.reshape(1, D).astype(F32))
    return out.reshape(B, Lp, D)


def _router_kernel(h_ref, wr_ref, gate_ref, rank_ref, cnt_s):
    first = (pl.program_id(0) == 0) & (pl.program_id(1) == 0)

    @pl.when(first)
    def _():
        cnt_s[...] = jnp.zeros_like(cnt_s)

    hv, wv = h_ref[...], wr_ref[...]
    h_hi, w_hi = hv.astype(BF16), wv.astype(BF16)
    h_lo = (hv - h_hi.astype(F32)).astype(BF16)
    w_lo = (wv - w_hi.astype(F32)).astype(BF16)
    both = _dot(h_hi, jnp.concatenate([w_hi, w_lo], axis=1))
    logits = both[:, :LANES] + (both[:, LANES:] + _dot(h_lo, w_hi))
    lane = lax.broadcasted_iota(jnp.int32, (TL, LANES), 1).astype(F32)
    neg = jnp.float32(-jnp.inf)
    logits = jnp.where(lane < N_EXPERTS, logits, neg)
    m1 = jnp.max(logits, axis=-1, keepdims=True)
    i1 = jnp.min(jnp.where(logits == m1, lane, float(LANES)), axis=-1, keepdims=True)
    oh1 = lane == i1
    rest = jnp.where(oh1, neg, logits)
    m2 = jnp.max(rest, axis=-1, keepdims=True)
    i2 = jnp.min(jnp.where(rest == m2, lane, float(LANES)), axis=-1, keepdims=True)
    oh2 = lane == i2
    e = jnp.exp(m2 - m1)
    w1 = 1.0 / (1.0 + e)
    w2 = e * w1
    gates = jnp.where(oh1, w1, 0.0) + jnp.where(oh2, w2, 0.0)
    gate_ref[...] = gates
    sel = (gates > 0.0).astype(BF16)
    ri = lax.broadcasted_iota(jnp.int32, (TL, TL), 0)
    ci = lax.broadcasted_iota(jnp.int32, (TL, TL), 1)
    before = _dot((ci < ri).astype(BF16), sel)
    cnt = cnt_s[...]
    rank_ref[...] = (before + cnt).astype(jnp.int32)
    cnt_s[...] = cnt + before[TL - 1:TL, :] + sel[TL - 1:TL, :].astype(F32)


def _router(h, w_router, seq):
    B, Lp, D = h.shape
    nj = seq // TL
    T = B * seq
    wr = jnp.pad(w_router.astype(F32), ((0, 0), (0, LANES - N_EXPERTS)))
    return pl.pallas_call(
        _router_kernel,
        grid=(B, nj),
        in_specs=[
            pl.BlockSpec((None, TL, D), lambda bi, j: (bi, j + 1, 0)),
            _const_spec((D, LANES)),
        ],
        out_specs=[
            pl.BlockSpec((TL, LANES), lambda bi, j: (bi * nj + j, 0)),
            pl.BlockSpec((TL, LANES), lambda bi, j: (bi * nj + j, 0)),
        ],
        out_shape=[jax.ShapeDtypeStruct((T, LANES), F32),
                   jax.ShapeDtypeStruct((T, LANES), jnp.int32)],
        scratch_shapes=[pltpu.VMEM((1, LANES), F32)],
        compiler_params=pltpu.CompilerParams(
            dimension_semantics=("arbitrary", "arbitrary")),
        name="moe_router",
    )(h, wr)


def _gather_steps(lo_be, hi_be, counts, tstart, texp, n_used, n_steps):
    nb, ne = lo_be.shape
    nt = texp.shape[0]
    tiles = jnp.arange(nt, dtype=jnp.int32)
    of_tile = texp[:, None] == jnp.arange(ne, dtype=jnp.int32)[None, :]
    pick_e = lambda a: jnp.sum(jnp.where(of_tile[:, :, None], a.T[None, :, :], 0), axis=1)
    pick_v = lambda v: jnp.sum(jnp.where(of_tile, v[None, :], 0), axis=1)
    s0 = (tiles - pick_v(tstart)) * TL
    s1 = jnp.minimum(s0 + TL, pick_v(counts))
    ov = (pick_e(lo_be) < s1[:, None]) & (pick_e(hi_be) > s0[:, None]) & (tiles < n_used)[:, None]
    nblk = jnp.sum(ov.astype(jnp.int32), axis=1)
    blk_ids = jnp.arange(nb, dtype=jnp.int32)
    b_first = jnp.min(jnp.where(ov, blk_ids[None, :], nb), axis=1)
    nstep = (nblk + GATHER_BLOCKS - 1) // GATHER_BLOCKS
    end = jnp.cumsum(nstep)
    total = end[-1]
    start = end - nstep
    s = jnp.arange(n_steps, dtype=jnp.int32)
    sc = jnp.minimum(s, total - 1)
    tile = jnp.sum((end[None, :] <= sc[:, None]).astype(jnp.int32), axis=1)
    hit = tile[:, None] == tiles[None, :]
    pick_t = lambda a: jnp.sum(jnp.where(hit, a[None, :], 0), axis=1)
    u = sc - pick_t(start)
    valid = s < total
    b0 = pick_t(b_first) + GATHER_BLOCKS * u
    b_last = pick_t(b_first + nblk - 1)
    nv = jnp.where(valid, jnp.minimum(GATHER_BLOCKS, pick_t(nblk) - GATHER_BLOCKS * u), 0)
    first = valid & (u == 0)
    last = valid & (u == pick_t(nstep) - 1)
    t_fill = n_used + (s - total)
    fill = jnp.logical_not(valid) & (t_fill < nt)
    tile_out = jnp.where(valid, tile, jnp.minimum(t_fill, nt - 1))
    mode = jnp.where(fill, 2, 0) + jnp.where(first, 1, 0)
    i32 = lambda a: a.astype(jnp.int32)
    return i32(tile_out), i32(b0), i32(b_last), i32(nv), i32(mode), i32(last), i32(pick_t(texp))


def _expert_up_kernel(tile_ref, b0_ref, bl_ref, nv_ref, mode_ref, last_ref, exp_ref, *refs, fc):
    nk = GATHER_BLOCKS
    h_refs, route_refs = refs[:nk], refs[nk:2 * nk]
    wg_ref, wu_ref, a_ref, gs_ref, x_s, g_s = refs[2 * nk:]
    p = pl.program_id(0)
    tile = tile_ref[p]
    e = exp_ref[p]

    @pl.when(mode_ref[p] == 2)
    def _():
        a_ref[...] = jnp.zeros_like(a_ref)
        gs_ref[...] = jnp.zeros_like(gs_ref)

    starts_tile = mode_ref[p] == 1

    for k in range(nk):
        @pl.when(k < nv_ref[p])
        def _(k=k):
            srow = route_refs[k][pl.ds(e, 1), :] - tile * TL
            grow = pltpu.bitcast(route_refs[k][pl.ds(N_EXPERTS + e, 1), :], F32)
            ri = lax.broadcasted_iota(jnp.int32, (TL, TL), 0)
            hit = ri == srow
            rows = _dot(hit.astype(BF16), h_refs[k][...]).astype(BF16)
            gate = jnp.sum(jnp.where(hit, grow, 0.0), axis=-1, keepdims=True)

            def accumulate():
                x_s[...] += rows
                g_s[...] += gate

            if k == 0:
                @pl.when(starts_tile)
                def _():
                    x_s[...] = rows
                    g_s[...] = jnp.broadcast_to(gate, g_s.shape)

                pl.when(jnp.logical_not(starts_tile))(accumulate)
            else:
                accumulate()

    @pl.when(last_ref[p] == 1)
    def _():
        xb = x_s[...]
        F = wg_ref.shape[1]
        for c in range(F // fc):
            cs = slice(c * fc, (c + 1) * fc)
            g = _dot(xb, wg_ref[:, cs])
            u = _dot(xb, wu_ref[:, cs])
            a_ref[:, cs] = (g * _sigmoid(g) * u).astype(BF16)
        gs_ref[...] = g_s[...]


def _expert_down_kernel(texp_ref, nused_ref, a_ref, gs_ref, wd_ref, y_ref):
    i = pl.program_id(0)

    @pl.when(i < nused_ref[0])
    def _():
        y_ref[...] = (_dot(a_ref[...], wd_ref[...]) * gs_ref[:, 0:1]).astype(BF16)

    @pl.when(i >= nused_ref[0])
    def _():
        y_ref[...] = jnp.zeros_like(y_ref)


def _combine_kernel(cidx_ref, cval_ref, *refs, alpha):
    npair = 2 * N_EXPERTS
    y_refs = refs[:npair]
    slot_ref, h_ref, g_ref, b_ref, out_ref, acc_s = refs[npair:]
    b = pl.program_id(0)
    ci = lax.broadcasted_iota(jnp.int32, (TL, TL), 1)

    def contrib(k):
        e = k // 2
        scol = slot_ref[:, e:e + 1] - cidx_ref[b * npair + k] * TL
        return _dot((ci == scol).astype(BF16), y_refs[k][...])

    acc = contrib(0)
    for e in range(1, N_EXPERTS):
        acc = acc + contrib(2 * e)
    acc_s[...] = acc
    for e in range(N_EXPERTS):
        @pl.when(cval_ref[b * N_EXPERTS + e] == 1)
        def _(e=e):
            acc_s[...] += contrib(2 * e + 1)

    res = alpha * h_ref[...] + acc_s[...]
    out_ref[...] = _layer_norm(res, g_ref[...], b_ref[...])


def _moe_layer(h, hb, w_router, wg, wu, wd, ln_g, ln_b, seq, *, alpha):
    B, Lp, D = h.shape
    E = N_EXPERTS
    F = wg.shape[-1]
    nj = seq // TL
    T = B * seq
    NB = T // TL
    NT = (2 * T) // TL + E
    NS = -(-(NB * E) // GATHER_BLOCKS) + NT

    gate, rank = _router(h, w_router, seq)

    gate8 = gate[:, :E]
    rank8 = rank[:, :E]
    sel8 = gate8 > 0.0
    counts = rank8[-1] + sel8[-1].astype(jnp.int32)
    ntile = (counts + TL - 1) // TL
    tstart = jnp.cumsum(ntile) - ntile
    n_used = jnp.sum(ntile).astype(jnp.int32).reshape(1)
    slot8 = jnp.where(sel8, tstart[None, :] * TL + rank8, -1).astype(jnp.int32)
    slot_tm = jnp.pad(slot8, ((0, 0), (0, LANES - E)), constant_values=-1)
    route = jnp.concatenate([slot8, lax.bitcast_convert_type(gate8, jnp.int32)], axis=1)
    route_em = route.reshape(NB, TL, 2 * E).transpose(0, 2, 1)
    tiles = jnp.arange(NT, dtype=jnp.int32)
    texp = jnp.sum((jnp.cumsum(ntile)[None, :] <= tiles[:, None]).astype(jnp.int32), axis=1)
    texp = jnp.minimum(texp, E - 1)

    cb = jnp.concatenate([rank8[::TL], counts[None, :]], axis=0)
    lo_be, hi_be = cb[:-1], cb[1:]
    blk_ids = jnp.arange(NB, dtype=jnp.int32)
    g_tile, g_b0, g_bl, g_nv, g_mode, g_last, g_exp = _gather_steps(
        lo_be, hi_be, counts, tstart, texp, n_used[0], NS)
    nonempty = hi_be > lo_be
    t0 = tstart[None, :] + lo_be // TL
    cross = nonempty & ((hi_be - 1) // TL > lo_be // TL)
    earlier = (blk_ids[None, :] <= blk_ids[:, None])[:, :, None]
    fill = lambda v: jnp.max(jnp.where(earlier, v[None, :, :], 0), axis=1)
    cidx = jnp.stack([fill(jnp.where(nonempty, t0, 0)), fill(jnp.where(cross, t0 + 1, 0))],
                     axis=-1).reshape(-1).astype(jnp.int32)
    cval = cross.reshape(-1).astype(jnp.int32)

    hblk = lambda blk: (blk // nj, blk % nj + 1, 0)
    src = lambda k: (lambda p, t, b0, bl, nv, f, l, x: jnp.minimum(b0[p] + k, bl[p]))
    of_step = lambda fn: (lambda p, t, b0, bl, nv, f, l, x: fn(t[p], x[p]))
    h_spec = lambda k: pl.BlockSpec((None, TL, D), lambda *a: hblk(src(k)(*a)))
    em_spec = lambda k: pl.BlockSpec((None, 2 * E, TL), lambda *a: (src(k)(*a), 0, 0))
    w_spec = pl.BlockSpec((None, D, F), of_step(lambda t, e: (e, 0, 0)))
    nk = GATHER_BLOCKS

    act, gslot = pl.pallas_call(
        functools.partial(_expert_up_kernel, fc=_ffn_chunk(F)),
        grid_spec=pltpu.PrefetchScalarGridSpec(
            num_scalar_prefetch=7,
            grid=(NS,),
            in_specs=([h_spec(k) for k in range(nk)] + [em_spec(k) for k in range(nk)]
                      + [w_spec, w_spec]),
            out_specs=[pl.BlockSpec((TL, F), of_step(lambda t, e: (t, 0))),
                       pl.BlockSpec((TL, LANES), of_step(lambda t, e: (t, 0)))],
            scratch_shapes=[pltpu.VMEM((TL, D), BF16), pltpu.VMEM((TL, LANES), F32)],
        ),
        out_shape=[jax.ShapeDtypeStruct((NT * TL, F), BF16),
                   jax.ShapeDtypeStruct((NT * TL, LANES), F32)],
        compiler_params=pltpu.CompilerParams(
            dimension_semantics=("arbitrary",), vmem_limit_bytes=VMEM_LIMIT),
        name="moe_expert_up",
    )(g_tile, g_b0, g_bl, g_nv, g_mode, g_last, g_exp, *([hb] * nk), *([route_em] * nk),
      wg.astype(BF16), wu.astype(BF16))

    used = lambda i, n: jnp.minimum(i, jnp.maximum(n[0] - 1, 0))
    ys = pl.pallas_call(
        _expert_down_kernel,
        grid_spec=pltpu.PrefetchScalarGridSpec(
            num_scalar_prefetch=2,
            grid=(NT,),
            in_specs=[
                pl.BlockSpec((TL, F), lambda i, x, n: (used(i, n), 0)),
                pl.BlockSpec((TL, LANES), lambda i, x, n: (used(i, n), 0)),
                pl.BlockSpec((None, F, D), lambda i, x, n: (x[used(i, n)], 0, 0)),
            ],
            out_specs=pl.BlockSpec((TL, D), lambda i, x, n: (i, 0)),
        ),
        out_shape=jax.ShapeDtypeStruct((NT * TL, D), BF16),
        compiler_params=pltpu.CompilerParams(
            dimension_semantics=("arbitrary",), vmem_limit_bytes=VMEM_LIMIT),
        name="moe_expert_down",
    )(texp, n_used, act, gslot, wd.astype(BF16))

    ypair = lambda k: pl.BlockSpec((TL, D), lambda b, ci, cv: (ci[b * 2 * E + k], 0))
    out = pl.pallas_call(
        functools.partial(_combine_kernel, alpha=alpha),
        grid_spec=pltpu.PrefetchScalarGridSpec(
            num_scalar_prefetch=2,
            grid=(NB,),
            in_specs=[ypair(k) for k in range(2 * E)] + [
                pl.BlockSpec((TL, LANES), lambda b, ci, cv: (b, 0)),
                pl.BlockSpec((None, TL, D), lambda b, ci, cv: hblk(b)),
                pl.BlockSpec((1, D), lambda b, ci, cv: (0, 0)),
                pl.BlockSpec((1, D), lambda b, ci, cv: (0, 0)),
            ],
            out_specs=pl.BlockSpec((None, TL, D), lambda b, ci, cv: (b // nj, b % nj, 0)),
            scratch_shapes=[pltpu.VMEM((TL, D), F32)],
        ),
        out_shape=jax.ShapeDtypeStruct((B, seq, D), F32),
        compiler_params=pltpu.CompilerParams(
            dimension_semantics=("arbitrary",), vmem_limit_bytes=VMEM_LIMIT),
        name="moe_combine",
    )(cidx, cval, *([ys] * (2 * E)), slot_tm, h,
      ln_g.reshape(1, D).astype(F32), ln_b.reshape(1, D).astype(F32))
    return out


def kernel(x, meta_tokens, ln_in_g, ln_in_b, w_in, lower_bounds, hg_norm_g, w_hg_out, conv_w,
           conv_b, conv_ln_g, conv_ln_b, w_conv_out, w_out, ln1_g, ln1_b, ffn_w_gate, ffn_w_up,
           ffn_w_down, moe_router, moe_w_gate, moe_w_up, moe_w_down, ln2_g, ln2_b):
    B, S, D = x.shape
    depth = w_in.shape[0]
    assert S % TL == 0 and D % LANES == 0 and meta_tokens.shape[0] == N_META
    alpha = float((2 * depth) ** 0.25)

    pr = jax.nn.softmax(lower_bounds.astype(F32), axis=0)
    lb_all = jnp.cumsum(pr, axis=0) - pr[0]

    tile_in = (x, meta_tokens, ln_in_g, ln_in_b)
    for l in range(depth):
        h, hb = _mix_layer(tile_in if l == 0 else (h,), w_in[l], lb_all[l], hg_norm_g[l], w_hg_out[l], conv_w[l], conv_b[l],
                           conv_ln_g[l], conv_ln_b[l], w_conv_out[l], w_out[l], ln1_g[l],
                           ln1_b[l], alpha=alpha)
        jl = l // 2
        last = l == depth - 1
        if l % 2 == 0:
            h = _dense_ffn(h, hb, ffn_w_gate[jl], ffn_w_up[jl], ffn_w_down[jl], ln2_g[l],
                           ln2_b[l], alpha=alpha)
            if last:
                h = h[:, TL:]
        else:
            assert last, "MoE layers other than the last one would need the padded layout back"
            h = _moe_layer(h, hb, moe_router[jl], moe_w_gate[jl], moe_w_up[jl], moe_w_down[jl],
                           ln2_g[l], ln2_b[l], S, alpha=alpha)
    return h
```

```python
import functools

import jax
import jax.numpy as jnp
from jax import lax
from jax.experimental import pallas as pl
from jax.experimental.pallas import tpu as pltpu

F32 = jnp.float32
BF16 = jnp.bfloat16

N_META = 16
HG_HEADS = 8
HG_DK = 128
CONV_WIDTH = 31
N_EXPERTS = 8
LN_EPS = 1e-5
RMS_EPS = 1e-6

LANES = 128
SUBLANES = 8
TL = 256
HIST = 32
CONV_ROWS = 64
GATHER_BLOCKS = 8
PROJ_COLS = 256
VMEM_LIMIT = 56 * 1024 * 1024
LOG2E = 1.4426950408889634


def _sigmoid(x):
    return 0.5 * jnp.tanh(0.5 * x) + 0.5


def _layer_norm(x, g, b):
    mu = jnp.mean(x, axis=-1, keepdims=True)
    xc = x - mu
    var = jnp.mean(xc * xc, axis=-1, keepdims=True)
    return xc * lax.rsqrt(var + LN_EPS) * g + b


def _dot(a, b):
    return jnp.dot(a, b, preferred_element_type=F32)


def _dot_nt(a, b):
    return lax.dot_general(a, b, (((1,), (1,)), ((), ())), preferred_element_type=F32)


def _dot_tn(a, b):
    return lax.dot_general(a, b, (((0,), (0,)), ((), ())), preferred_element_type=F32)


def _const_spec(shape):
    nd = len(shape)
    return pl.BlockSpec(shape, lambda *_: (0,) * nd, pipeline_mode=pl.Buffered(1))


def _cumsum_rows(x):
    n = x.shape[0]
    hi = x.astype(BF16)
    r1 = x - hi.astype(F32)
    mid = r1.astype(BF16)
    lo = (r1 - mid.astype(F32)).astype(BF16)
    ri = lax.broadcasted_iota(jnp.int32, (n, n), 0)
    ci = lax.broadcasted_iota(jnp.int32, (n, n), 1)
    tri = (ri >= ci).astype(BF16)
    return _dot(tri, hi) + _dot(tri, mid) + _dot(tri, lo)


def _level_scores(qh, ch, bh, m):
    n = qh.shape[0]
    zero = jnp.zeros((m, qh.shape[1]), F32)
    qp, kp = [], []
    for jb in range(n // (2 * m)):
        lo, mid, hi = jb * 2 * m, jb * 2 * m + m, (jb + 1) * 2 * m
        e = bh[mid - 1:mid, :]
        kp += [jnp.exp2(e + ch[lo:mid]), zero]
        qp += [zero, qh[mid:hi] * jnp.exp2(bh[mid:hi] - e)]
    qp = jnp.concatenate(qp, axis=0).astype(BF16)
    kp = jnp.concatenate(kp, axis=0).astype(BF16)
    return _dot_nt(qp, kp)


def _mix_kernel(*refs, pad, alpha, fuse_ln_in):
    n_in = 4 if fuse_ln_in else 1
    tile_in, refs = refs[:n_in], refs[n_in:]
    out_ref, outb_ref = refs[12], refs[13]
    st_s, hist_s = refs[14], refs[15]
    hv_s, stm_s, histm_s = refs[-3:]
    b = pl.program_id(0)
    j = pl.program_id(1)
    first = j == 0

    @pl.when(first & (b == 0))
    def _():
        st_s[...] = jnp.zeros_like(st_s)
        hist_s[0:HIST, :] = jnp.zeros((HIST, hist_s.shape[1]), F32)

    @pl.when(first & (b > 0))
    def _():
        st_s[...] = stm_s[...]
        hist_s[0:HIST, :] = histm_s[...]
        out_ref[...] = jnp.zeros_like(out_ref)
        outb_ref[...] = jnp.zeros_like(outb_ref)

    @pl.when(jnp.logical_not(first) | (b == 0))
    def _():
        rows = lax.broadcasted_iota(jnp.int32, (TL, 1), 0)
        if fuse_ln_in:
            x_ref, meta_ref, g_ref, b_ref = tile_in

            @pl.when(first)
            def _():
                y = _layer_norm(meta_ref[...], g_ref[...], b_ref[...])
                hv_s[...] = jnp.where(rows >= pad, y, 0.0)

            @pl.when(jnp.logical_not(first))
            def _():
                hv_s[...] = _layer_norm(x_ref[...], g_ref[...], b_ref[...])
        else:
            hv_s[...] = jnp.where(j * TL + rows >= pad, tile_in[0][...], 0.0)

        _mix_tile(*refs[:-2], alpha=alpha)

        @pl.when(first)
        def _():
            stm_s[...] = st_s[...]
            histm_s[...] = hist_s[0:HIST, :]


def _mix_tile(w_in_ref, lb_ref, hgg_ref, w_hg_ref, cw_ref, cb_ref, clg_ref, clb_ref,
              w_cv_ref, w_out_ref, ln_g_ref, ln_b_ref, out_ref, outb_ref,
              st_s, hist_s, hb_s, q_s, c_s, b_s, v_s, og_s, ob_s, sh_s, gate_s, hv_s, *, alpha):
    D = hv_s.shape[-1]
    H = HG_HEADS
    HALF = TL // 2
    hb_s[...] = hv_s[...].astype(BF16)

    def proj(g, c0=0, width=None):
        lo = g * D + c0
        return _dot(hb_s[...], w_in_ref[:, lo:lo + (D if width is None else width)])


    for c0 in range(0, D, PROJ_COLS):
        cols = slice(c0, c0 + PROJ_COLS)

        def split_heads(dst, val):
            for i in range(PROJ_COLS // LANES):
                dst[c0 // LANES + i] = val[:, i * LANES:(i + 1) * LANES]

        p = proj(0, c0, PROJ_COLS)
        split_heads(q_s, p * (jnp.tanh(p) + 1.0))

        z = proj(1, c0, PROJ_COLS)
        lb = lb_ref[:, cols]
        one_m_lb = 1.0 - lb
        e = jnp.exp2(jnp.abs(z) * (-LOG2E))
        log_sz = jnp.minimum(z, 0.0) - jnp.log(1.0 + e)
        f = lb + one_m_lb * jnp.exp(log_sz)
        log_f = jnp.where(lb > 0.0, jnp.log(f), log_sz)
        log_k = (log_sz - z) + jnp.log(one_m_lb)
        bq = _cumsum_rows(log_f) * LOG2E
        split_heads(b_s, bq)
        split_heads(c_s, log_k * LOG2E - bq)

        split_heads(v_s, proj(2, c0, PROJ_COLS))
        p = proj(3, c0, PROJ_COLS)
        split_heads(og_s, p * (jnp.tanh(p) + 1.0))
        hist_s[HIST:HIST + TL, cols] = (proj(4, c0, PROJ_COLS)
                                        * (jnp.tanh(proj(5, c0, PROJ_COLS)) + 1.0))

    ri = lax.broadcasted_iota(jnp.int32, (HALF, HALF), 0)
    ci = lax.broadcasted_iota(jnp.int32, (HALF, HALF), 1)
    lane8 = lax.broadcasted_iota(jnp.int32, (SUBLANES, LANES), 1)
    sub8 = lax.broadcasted_iota(jnp.int32, (SUBLANES, LANES), 0)

    def head_body(hd, carry):
        q = q_s[hd]
        bc = b_s[hd]
        cc = c_s[hd]
        vb = v_s[hd].astype(BF16)
        st = st_s[hd]
        b_last = bc[TL - 1:TL, :]

        o_inter = _dot_nt((q * jnp.exp2(bc)).astype(BF16), st.astype(BF16))
        k_end = jnp.exp2(b_last + cc).astype(BF16)
        st_s[hd] = st * jnp.exp2(b_last) + _dot_tn(vb, k_end)

        def diag_block(r0):
            q8, b8 = q[r0:r0 + SUBLANES], bc[r0:r0 + SUBLANES]
            c0 = r0 % HALF
            acc = jnp.zeros((SUBLANES, LANES), F32)
            for s in range(SUBLANES):
                w = jnp.exp2(b8 + c_s[hd, r0 + s:r0 + s + 1, :])
                col = jnp.sum(q8 * w, axis=-1, keepdims=True)
                acc = jnp.where(lane8 == c0 + s, col, acc)
            keep = (lane8 >= c0) & (lane8 - c0 <= sub8)
            return jnp.where(keep, acc, 0.0)

        b_mid = bc[HALF - 1:HALF, :]
        a10 = _dot_nt((q[HALF:] * jnp.exp2(bc[HALF:] - b_mid)).astype(BF16),
                      jnp.exp2(b_mid + cc[:HALF]).astype(BF16))
        atts = []
        for hh in range(2):
            sl = slice(hh * HALF, (hh + 1) * HALF)
            qh, ch, bh = q[sl], cc[sl], bc[sl]
            att = jnp.concatenate(
                [diag_block(hh * HALF + i * SUBLANES) for i in range(HALF // SUBLANES)], axis=0)
            m = HALF // 2
            while m >= SUBLANES:
                sib = (ri & -(2 * m)) == (ci & -(2 * m))
                att = att + jnp.where(sib, _level_scores(qh, ch, bh, m), 0.0)
                m //= 2
            atts.append(att)
        o0 = _dot(atts[0].astype(BF16), vb[:HALF])
        o1 = _dot(jnp.concatenate([a10, atts[1]], axis=1).astype(BF16), vb)
        o = o_inter + jnp.concatenate([o0, o1], axis=0)

        ms = jnp.mean(o * o, axis=-1, keepdims=True)
        ob_s[hd] = (o * lax.rsqrt(ms + RMS_EPS) * hgg_ref[hd] * og_s[hd]).astype(BF16)
        return carry

    per_gate = D // PROJ_COLS
    for hd in range(H):
        head_body(hd, 0)
        for unit in range(hd * 2 * per_gate // H, (hd + 1) * 2 * per_gate // H):
            which, c0 = unit // per_gate, (unit % per_gate) * PROJ_COLS
            gate_s[which, :, c0:c0 + PROJ_COLS] = jnp.tanh(proj(6 + which, c0, PROJ_COLS)) + 1.0

    y_rec = _dot(jnp.concatenate([ob_s[hd] for hd in range(H)], axis=1), w_hg_ref[...])

    base = HIST - (CONV_WIDTH - 1)
    nshift = HIST + TL - SUBLANES
    strips = []
    for c in range(D // LANES):
        cs = slice(c * LANES, (c + 1) * LANES)
        for r in range(1, SUBLANES):
            sh_s[r - 1] = hist_s[r:r + nshift, cs]
        chunks = []
        for rc in range(0, TL, CONV_ROWS):
            acc = jnp.broadcast_to(cb_ref[:, cs], (CONV_ROWS, LANES))
            for t in range(CONV_WIDTH):
                a, r = divmod(base + t, SUBLANES)
                lo = a * SUBLANES + rc
                src = hist_s[lo:lo + CONV_ROWS, cs] if r == 0 else sh_s[r - 1, lo:lo + CONV_ROWS, :]
                acc = acc + src * cw_ref[t:t + 1, cs]
            chunks.append(acc)
        strips.append(jnp.concatenate(chunks, axis=0))
    yc = jnp.concatenate(strips, axis=1)
    hist_s[0:HIST, :] = hist_s[TL:TL + HIST, :]
    yc = _layer_norm(yc, clg_ref[...], clb_ref[...])
    y_conv = _dot((yc * (jnp.tanh(yc) + 1.0)).astype(BF16), w_cv_ref[...])

    mixed = gate_s[0] * y_rec + gate_s[1] * y_conv
    res = alpha * hv_s[...] + _dot(mixed.astype(BF16), w_out_ref[...])
    hn = _layer_norm(res, ln_g_ref[...], ln_b_ref[...])
    out_ref[...] = hn
    outb_ref[...] = hn.astype(BF16)


def _mix_layer(tile_in, w_in, lb, hg_g, w_hg, conv_w, conv_b, cln_g, cln_b, w_cv, w_out, ln_g,
               ln_b, *, alpha):
    row = lambda a: a.reshape(1, -1).astype(F32)
    fuse_ln_in = len(tile_in) == 4
    if fuse_ln_in:
        x, meta_tokens, g_in, b_in = tile_in
        B, S, D = x.shape
        Lp = S + TL
        meta_pad = jnp.pad(meta_tokens.astype(F32), ((TL - N_META, 0), (0, 0)))
        tile_args = (x, meta_pad, row(g_in), row(b_in))
        tile_specs = [
            pl.BlockSpec((None, TL, D), lambda bi, j: (bi, jnp.maximum(j - 1, 0), 0)),
            _const_spec((TL, D)), _const_spec((1, D)), _const_spec((1, D))]
    else:
        B, Lp, D = tile_in[0].shape
        tile_args = tile_in
        tile_specs = [pl.BlockSpec((None, TL, D), lambda bi, j: (bi, j, 0))]
    H = HG_HEADS
    nt = Lp // TL
    col_half = jnp.concatenate([
        jnp.full((D,), 0.5, F32), jnp.ones((2 * D,), F32), jnp.full((5 * D,), 0.5, F32)])
    w_in = w_in * col_half
    cln_g, cln_b, w_out = 0.5 * cln_g, 0.5 * cln_b, 0.5 * w_out
    return pl.pallas_call(
        functools.partial(_mix_kernel, pad=TL - N_META, alpha=alpha, fuse_ln_in=fuse_ln_in),
        grid=(B, nt),
        in_specs=tile_specs + [
            _const_spec(w_in.shape),
            _const_spec((1, D)),
            _const_spec((H, 1, LANES)),
            _const_spec((D, D)),
            _const_spec((CONV_WIDTH, D)),
            _const_spec((1, D)),
            _const_spec((1, D)),
            _const_spec((1, D)),
            _const_spec((D, D)),
            _const_spec((D, D)),
            _const_spec((1, D)),
            _const_spec((1, D)),
        ],
        out_specs=[pl.BlockSpec((None, TL, D), lambda bi, j: (bi, j, 0)),
                   pl.BlockSpec((None, TL, D), lambda bi, j: (bi, j, 0))],
        out_shape=[jax.ShapeDtypeStruct((B, Lp, D), F32),
                   jax.ShapeDtypeStruct((B, Lp, D), BF16)],
        scratch_shapes=[
            pltpu.VMEM((H, HG_DK, HG_DK), F32),
            pltpu.VMEM((HIST + TL, D), F32),
            pltpu.VMEM((TL, D), BF16),
            pltpu.VMEM((H, TL, LANES), F32),
            pltpu.VMEM((H, TL, LANES), F32),
            pltpu.VMEM((H, TL, LANES), F32),
            pltpu.VMEM((H, TL, LANES), F32),
            pltpu.VMEM((H, TL, LANES), F32),
            pltpu.VMEM((H, TL, LANES), BF16),
            pltpu.VMEM((SUBLANES - 1, HIST + TL - SUBLANES, LANES), F32),
            pltpu.VMEM((2, TL, D), F32),
            pltpu.VMEM((TL, D), F32),
            pltpu.VMEM((H, HG_DK, HG_DK), F32),
            pltpu.VMEM((HIST, D), F32),
        ],
        compiler_params=pltpu.CompilerParams(
            dimension_semantics=("arbitrary", "arbitrary"),
            vmem_limit_bytes=VMEM_LIMIT),
        name="mix_layer",
    )(*tile_args, w_in.astype(BF16), row(lb), hg_g.reshape(H, 1, LANES).astype(F32), w_hg.astype(BF16),
      conv_w.astype(F32), row(conv_b), row(cln_g), row(cln_b), w_cv.astype(BF16),
      w_out.astype(BF16), row(ln_g), row(ln_b))


def _ffn_kernel(h_ref, hb_ref, wg_ref, wu_ref, wd_ref, g_ref, b_ref, out_ref, a_s, *, alpha, fc,
                tiles_per_seq):
    i = pl.program_id(0)
    dead = (i % tiles_per_seq == 0) & (i > 0)

    @pl.when(dead)
    def _():
        out_ref[...] = jnp.zeros_like(out_ref)

    @pl.when(jnp.logical_not(dead))
    def _():
        xb = hb_ref[...]
        F = wg_ref.shape[1]
        for c in range(F // fc):
            cs = slice(c * fc, (c + 1) * fc)
            g = _dot(xb, wg_ref[:, cs])
            u = _dot(xb, wu_ref[:, cs])
            a_s[:, cs] = (g * _sigmoid(g) * u).astype(BF16)
        res = alpha * h_ref[...] + _dot(a_s[...], wd_ref[...])
        out_ref[...] = _layer_norm(res, g_ref[...], b_ref[...])


def _ffn_chunk(F):
    for fc in (256, 128):
        if F % fc == 0:
            return fc
    return F


def _dense_ffn(h, hb, wg, wu, wd, ln_g, ln_b, *, alpha):
    B, Lp, D = h.shape
    F = wg.shape[1]
    R = B * Lp
    out = pl.pallas_call(
        functools.partial(_ffn_kernel, alpha=alpha, fc=_ffn_chunk(F), tiles_per_seq=Lp // TL),
        grid=(R // TL,),
        in_specs=[
            pl.BlockSpec((TL, D), lambda i: (i, 0)),
            pl.BlockSpec((TL, D), lambda i: (i, 0)),
            _const_spec((D, F)),
            _const_spec((D, F)),
            _const_spec((F, D)),
            _const_spec((1, D)),
            _const_spec((1, D)),
        ],
        out_specs=pl.BlockSpec((TL, D), lambda i: (i, 0)),
        out_shape=jax.ShapeDtypeStruct((R, D), F32),
        scratch_shapes=[pltpu.VMEM((TL, F), BF16)],
        compiler_params=pltpu.CompilerParams(
            dimension_semantics=("arbitrary",), vmem_limit_bytes=VMEM_LIMIT),
        name="dense_ffn",
    )(h.reshape(R, D), hb.reshape(R, D), wg.astype(BF16), wu.astype(BF16), wd.astype(BF16),
      ln_g.reshape(1, D).astype(F32), ln_b.reshape(1, D).astype(F32))
    return out.reshape(B, Lp, D)


def _router_kernel(h_ref, wr_ref, gate_ref, rank_ref, cnt_s):
    first = (pl.program_id(0) == 0) & (pl.program_id(1) == 0)

    @pl.when(first)
    def _():
        cnt_s[...] = jnp.zeros_like(cnt_s)

    hv, wv = h_ref[...], wr_ref[...]
    h_hi, w_hi = hv.astype(BF16), wv.astype(BF16)
    h_lo = (hv - h_hi.astype(F32)).astype(BF16)
    w_lo = (wv - w_hi.astype(F32)).astype(BF16)
    both = _dot(h_hi, jnp.concatenate([w_hi, w_lo], axis=1))
    logits = both[:, :LANES] + (both[:, LANES:] + _dot(h_lo, w_hi))
    lane = lax.broadcasted_iota(jnp.int32, (TL, LANES), 1).astype(F32)
    neg = jnp.float32(-jnp.inf)
    logits = jnp.where(lane < N_EXPERTS, logits, neg)
    m1 = jnp.max(logits, axis=-1, keepdims=True)
    i1 = jnp.min(jnp.where(logits == m1, lane, float(LANES)), axis=-1, keepdims=True)
    oh1 = lane == i1
    rest = jnp.where(oh1, neg, logits)
    m2 = jnp.max(rest, axis=-1, keepdims=True)
    i2 = jnp.min(jnp.where(rest == m2, lane, float(LANES)), axis=-1, keepdims=True)
    oh2 = lane == i2
    e = jnp.exp(m2 - m1)
    w1 = 1.0 / (1.0 + e)
    w2 = e * w1
    gates = jnp.where(oh1, w1, 0.0) + jnp.where(oh2, w2, 0.0)
    gate_ref[...] = gates
    sel = (gates > 0.0).astype(BF16)
    ri = lax.broadcasted_iota(jnp.int32, (TL, TL), 0)
    ci = lax.broadcasted_iota(jnp.int32, (TL, TL), 1)
    before = _dot((ci < ri).astype(BF16), sel)
    cnt = cnt_s[...]
    rank_ref[...] = (before + cnt).astype(jnp.int32)
    cnt_s[...] = cnt + before[TL - 1:TL, :] + sel[TL - 1:TL, :].astype(F32)


def _router(h, w_router, seq):
    B, Lp, D = h.shape
    nj = seq // TL
    T = B * seq
    wr = jnp.pad(w_router.astype(F32), ((0, 0), (0, LANES - N_EXPERTS)))
    return pl.pallas_call(
        _router_kernel,
        grid=(B, nj),
        in_specs=[
            pl.BlockSpec((None, TL, D), lambda bi, j: (bi, j + 1, 0)),
            _const_spec((D, LANES)),
        ],
        out_specs=[
            pl.BlockSpec((TL, LANES), lambda bi, j: (bi * nj + j, 0)),
            pl.BlockSpec((TL, LANES), lambda bi, j: (bi * nj + j, 0)),
        ],
        out_shape=[jax.ShapeDtypeStruct((T, LANES), F32),
                   jax.ShapeDtypeStruct((T, LANES), jnp.int32)],
        scratch_shapes=[pltpu.VMEM((1, LANES), F32)],
        compiler_params=pltpu.CompilerParams(
            dimension_semantics=("arbitrary", "arbitrary")),
        name="moe_router",
    )(h, wr)


def _gather_steps(lo_be, hi_be, counts, tstart, texp, n_used, n_steps):
    nb, ne = lo_be.shape
    nt = texp.shape[0]
    tiles = jnp.arange(nt, dtype=jnp.int32)
    of_tile = texp[:, None] == jnp.arange(ne, dtype=jnp.int32)[None, :]
    pick_e = lambda a: jnp.sum(jnp.where(of_tile[:, :, None], a.T[None, :, :], 0), axis=1)
    pick_v = lambda v: jnp.sum(jnp.where(of_tile, v[None, :], 0), axis=1)
    s0 = (tiles - pick_v(tstart)) * TL
    s1 = jnp.minimum(s0 + TL, pick_v(counts))
    ov = (pick_e(lo_be) < s1[:, None]) & (pick_e(hi_be) > s0[:, None]) & (tiles < n_used)[:, None]
    nblk = jnp.sum(ov.astype(jnp.int32), axis=1)
    blk_ids = jnp.arange(nb, dtype=jnp.int32)
    b_first = jnp.min(jnp.where(ov, blk_ids[None, :], nb), axis=1)
    nstep = (nblk + GATHER_BLOCKS - 1) // GATHER_BLOCKS
    end = jnp.cumsum(nstep)
    total = end[-1]
    start = end - nstep
    s = jnp.arange(n_steps, dtype=jnp.int32)
    sc = jnp.minimum(s, total - 1)
    tile = jnp.sum((end[None, :] <= sc[:, None]).astype(jnp.int32), axis=1)
    hit = tile[:, None] == tiles[None, :]
    pick_t = lambda a: jnp.sum(jnp.where(hit, a[None, :], 0), axis=1)
    u = sc - pick_t(start)
    valid = s < total
    b0 = pick_t(b_first) + GATHER_BLOCKS * u
    b_last = pick_t(b_first + nblk - 1)
    nv = jnp.where(valid, jnp.minimum(GATHER_BLOCKS, pick_t(nblk) - GATHER_BLOCKS * u), 0)
    first = valid & (u == 0)
    last = valid & (u == pick_t(nstep) - 1)
    t_fill = n_used + (s - total)
    fill = jnp.logical_not(valid) & (t_fill < nt)
    tile_out = jnp.where(valid, tile, jnp.minimum(t_fill, nt - 1))
    mode = jnp.where(fill, 2, 0) + jnp.where(first, 1, 0)
    i32 = lambda a: a.astype(jnp.int32)
    return i32(tile_out), i32(b0), i32(b_last), i32(nv), i32(mode), i32(last), i32(pick_t(texp))


def _expert_up_kernel(tile_ref, b0_ref, bl_ref, nv_ref, mode_ref, last_ref, exp_ref, *refs, fc):
    nk = GATHER_BLOCKS
    h_refs, route_refs = refs[:nk], refs[nk:2 * nk]
    wg_ref, wu_ref, a_ref, gs_ref, x_s, g_s = refs[2 * nk:]
    p = pl.program_id(0)
    tile = tile_ref[p]
    e = exp_ref[p]

    @pl.when(mode_ref[p] == 2)
    def _():
        a_ref[...] = jnp.zeros_like(a_ref)
        gs_ref[...] = jnp.zeros_like(gs_ref)

    starts_tile = mode_ref[p] == 1

    for k in range(nk):
        @pl.when(k < nv_ref[p])
        def _(k=k):
            srow = route_refs[k][pl.ds(e, 1), :] - tile * TL
            grow = pltpu.bitcast(route_refs[k][pl.ds(N_EXPERTS + e, 1), :], F32)
            ri = lax.broadcasted_iota(jnp.int32, (TL, TL), 0)
            hit = ri == srow
            rows = _dot(hit.astype(BF16), h_refs[k][...]).astype(BF16)
            gate = jnp.sum(jnp.where(hit, grow, 0.0), axis=-1, keepdims=True)

            def accumulate():
                x_s[...] += rows
                g_s[...] += gate

            if k == 0:
                @pl.when(starts_tile)
                def _():
                    x_s[...] = rows
                    g_s[...] = jnp.broadcast_to(gate, g_s.shape)

                pl.when(jnp.logical_not(starts_tile))(accumulate)
            else:
                accumulate()

    @pl.when(last_ref[p] == 1)
    def _():
        xb = x_s[...]
        F = wg_ref.shape[1]
        for c in range(F // fc):
            cs = slice(c * fc, (c + 1) * fc)
            g = _dot(xb, wg_ref[:, cs])
            u = _dot(xb, wu_ref[:, cs])
            a_ref[:, cs] = (g * _sigmoid(g) * u).astype(BF16)
        gs_ref[...] = g_s[...]


def _expert_down_kernel(texp_ref, nused_ref, a_ref, gs_ref, wd_ref, y_ref, wd_s):
    i = pl.program_id(0)
    live = i < nused_ref[0]

    @pl.when(live & ((i == 0) | (texp_ref[i] != texp_ref[jnp.maximum(i - 1, 0)])))
    def _():
        wd_s[...] = wd_ref[...].astype(BF16)

    @pl.when(live)
    def _():
        y_ref[...] = (_dot(a_ref[...], wd_s[...]) * gs_ref[:, 0:1]).astype(BF16)

    @pl.when(i >= nused_ref[0])
    def _():
        y_ref[...] = jnp.zeros_like(y_ref)


def _combine_kernel(cidx_ref, cval_ref, *refs, alpha):
    npair = 2 * N_EXPERTS
    y_refs = refs[:npair]
    slot_ref, h_ref, g_ref, b_ref, out_ref, acc_s = refs[npair:]
    b = pl.program_id(0)
    ci = lax.broadcasted_iota(jnp.int32, (TL, TL), 1)

    def contrib(k):
        e = k // 2
        scol = slot_ref[:, e:e + 1] - cidx_ref[b * npair + k] * TL
        return _dot((ci == scol).astype(BF16), y_refs[k][...])

    acc = contrib(0)
    for e in range(1, N_EXPERTS):
        acc = acc + contrib(2 * e)
    acc_s[...] = acc
    for e in range(N_EXPERTS):
        @pl.when(cval_ref[b * N_EXPERTS + e] == 1)
        def _(e=e):
            acc_s[...] += contrib(2 * e + 1)

    res = alpha * h_ref[...] + acc_s[...]
    out_ref[...] = _layer_norm(res, g_ref[...], b_ref[...])


def _moe_layer(h, hb, w_router, wg, wu, wd, ln_g, ln_b, seq, *, alpha):
    B, Lp, D = h.shape
    E = N_EXPERTS
    F = wg.shape[-1]
    nj = seq // TL
    T = B * seq
    NB = T // TL
    NT = (2 * T) // TL + E
    NS = -(-(NB * E) // GATHER_BLOCKS) + NT

    gate, rank = _router(h, w_router, seq)

    gate8 = gate[:, :E]
    rank8 = rank[:, :E]
    sel8 = gate8 > 0.0
    counts = rank8[-1] + sel8[-1].astype(jnp.int32)
    ntile = (counts + TL - 1) // TL
    tstart = jnp.cumsum(ntile) - ntile
    n_used = jnp.sum(ntile).astype(jnp.int32).reshape(1)
    slot8 = jnp.where(sel8, tstart[None, :] * TL + rank8, -1).astype(jnp.int32)
    slot_tm = jnp.pad(slot8, ((0, 0), (0, LANES - E)), constant_values=-1)
    route = jnp.concatenate([slot8, lax.bitcast_convert_type(gate8, jnp.int32)], axis=1)
    route_em = route.reshape(NB, TL, 2 * E).transpose(0, 2, 1)
    tiles = jnp.arange(NT, dtype=jnp.int32)
    texp = jnp.sum((jnp.cumsum(ntile)[None, :] <= tiles[:, None]).astype(jnp.int32), axis=1)
    texp = jnp.minimum(texp, E - 1)

    cb = jnp.concatenate([rank8[::TL], counts[None, :]], axis=0)
    lo_be, hi_be = cb[:-1], cb[1:]
    blk_ids = jnp.arange(NB, dtype=jnp.int32)
    g_tile, g_b0, g_bl, g_nv, g_mode, g_last, g_exp = _gather_steps(
        lo_be, hi_be, counts, tstart, texp, n_used[0], NS)
    nonempty = hi_be > lo_be
    t0 = tstart[None, :] + lo_be // TL
    cross = nonempty & ((hi_be - 1) // TL > lo_be // TL)
    earlier = (blk_ids[None, :] <= blk_ids[:, None])[:, :, None]
    fill = lambda v: jnp.max(jnp.where(earlier, v[None, :, :], 0), axis=1)
    cidx = jnp.stack([fill(jnp.where(nonempty, t0, 0)), fill(jnp.where(cross, t0 + 1, 0))],
                     axis=-1).reshape(-1).astype(jnp.int32)
    cval = cross.reshape(-1).astype(jnp.int32)

    hblk = lambda blk: (blk // nj, blk % nj + 1, 0)
    src = lambda k: (lambda p, t, b0, bl, nv, f, l, x: jnp.minimum(b0[p] + k, bl[p]))
    of_step = lambda fn: (lambda p, t, b0, bl, nv, f, l, x: fn(t[p], x[p]))
    h_spec = lambda k: pl.BlockSpec((None, TL, D), lambda *a: hblk(src(k)(*a)))
    em_spec = lambda k: pl.BlockSpec((None, 2 * E, TL), lambda *a: (src(k)(*a), 0, 0))
    w_spec = pl.BlockSpec((None, D, F), of_step(lambda t, e: (e, 0, 0)))
    nk = GATHER_BLOCKS

    act, gslot = pl.pallas_call(
        functools.partial(_expert_up_kernel, fc=_ffn_chunk(F)),
        grid_spec=pltpu.PrefetchScalarGridSpec(
            num_scalar_prefetch=7,
            grid=(NS,),
            in_specs=([h_spec(k) for k in range(nk)] + [em_spec(k) for k in range(nk)]
                      + [w_spec, w_spec]),
            out_specs=[pl.BlockSpec((TL, F), of_step(lambda t, e: (t, 0))),
                       pl.BlockSpec((TL, LANES), of_step(lambda t, e: (t, 0)))],
            scratch_shapes=[pltpu.VMEM((TL, D), BF16), pltpu.VMEM((TL, LANES), F32)],
        ),
        out_shape=[jax.ShapeDtypeStruct((NT * TL, F), BF16),
                   jax.ShapeDtypeStruct((NT * TL, LANES), F32)],
        compiler_params=pltpu.CompilerParams(
            dimension_semantics=("arbitrary",), vmem_limit_bytes=VMEM_LIMIT),
        name="moe_expert_up",
    )(g_tile, g_b0, g_bl, g_nv, g_mode, g_last, g_exp, *([hb] * nk), *([route_em] * nk),
      wg.astype(BF16), wu.astype(BF16))

    used = lambda i, n: jnp.minimum(i, jnp.maximum(n[0] - 1, 0))
    ys = pl.pallas_call(
        _expert_down_kernel,
        grid_spec=pltpu.PrefetchScalarGridSpec(
            num_scalar_prefetch=2,
            grid=(NT,),
            in_specs=[
                pl.BlockSpec((TL, F), lambda i, x, n: (used(i, n), 0)),
                pl.BlockSpec((TL, LANES), lambda i, x, n: (used(i, n), 0)),
                pl.BlockSpec((None, F, D), lambda i, x, n: (x[used(i, n)], 0, 0)),
            ],
            out_specs=pl.BlockSpec((TL, D), lambda i, x, n: (i, 0)),
            scratch_shapes=[pltpu.VMEM((F, D), BF16)],
        ),
        out_shape=jax.ShapeDtypeStruct((NT * TL, D), BF16),
        compiler_params=pltpu.CompilerParams(
            dimension_semantics=("arbitrary",), vmem_limit_bytes=VMEM_LIMIT),
        name="moe_expert_down",
    )(texp, n_used, act, gslot, wd.astype(F32))

    ypair = lambda k: pl.BlockSpec((TL, D), lambda b, ci, cv: (ci[b * 2 * E + k], 0))
    out = pl.pallas_call(
        functools.partial(_combine_kernel, alpha=alpha),
        grid_spec=pltpu.PrefetchScalarGridSpec(
            num_scalar_prefetch=2,
            grid=(NB,),
            in_specs=[ypair(k) for k in range(2 * E)] + [
                pl.BlockSpec((TL, LANES), lambda b, ci, cv: (b, 0)),
                pl.BlockSpec((None, TL, D), lambda b, ci, cv: hblk(b)),
                pl.BlockSpec((1, D), lambda b, ci, cv: (0, 0)),
                pl.BlockSpec((1, D), lambda b, ci, cv: (0, 0)),
            ],
            out_specs=pl.BlockSpec((None, TL, D), lambda b, ci, cv: (b // nj, b % nj, 0)),
            scratch_shapes=[pltpu.VMEM((TL, D), F32)],
        ),
        out_shape=jax.ShapeDtypeStruct((B, seq, D), F32),
        compiler_params=pltpu.CompilerParams(
            dimension_semantics=("arbitrary",), vmem_limit_bytes=VMEM_LIMIT),
        name="moe_combine",
    )(cidx, cval, *([ys] * (2 * E)), slot_tm, h,
      ln_g.reshape(1, D).astype(F32), ln_b.reshape(1, D).astype(F32))
    return out


def kernel(x, meta_tokens, ln_in_g, ln_in_b, w_in, lower_bounds, hg_norm_g, w_hg_out, conv_w,
           conv_b, conv_ln_g, conv_ln_b, w_conv_out, w_out, ln1_g, ln1_b, ffn_w_gate, ffn_w_up,
           ffn_w_down, moe_router, moe_w_gate, moe_w_up, moe_w_down, ln2_g, ln2_b):
    B, S, D = x.shape
    depth = w_in.shape[0]
    assert S % TL == 0 and D % LANES == 0 and meta_tokens.shape[0] == N_META
    alpha = float((2 * depth) ** 0.25)

    pr = jax.nn.softmax(lower_bounds.astype(F32), axis=0)
    lb_all = jnp.cumsum(pr, axis=0) - pr[0]

    tile_in = (x, meta_tokens, ln_in_g, ln_in_b)
    for l in range(depth):
        h, hb = _mix_layer(tile_in if l == 0 else (h,), w_in[l], lb_all[l], hg_norm_g[l], w_hg_out[l], conv_w[l], conv_b[l],
                           conv_ln_g[l], conv_ln_b[l], w_conv_out[l], w_out[l], ln1_g[l],
                           ln1_b[l], alpha=alpha)
        jl = l // 2
        last = l == depth - 1
        if l % 2 == 0:
            h = _dense_ffn(h, hb, ffn_w_gate[jl], ffn_w_up[jl], ffn_w_down[jl], ln2_g[l],
                           ln2_b[l], alpha=alpha)
            if last:
                h = h[:, TL:]
        else:
            assert last, "MoE layers other than the last one would need the padded layout back"
            h = _moe_layer(h, hb, moe_router[jl], moe_w_gate[jl], moe_w_up[jl], moe_w_down[jl],
                           ln2_g[l], ln2_b[l], S, alpha=alpha)
    return h
```
